```python
import jax, jax.numpy as jnp
from jax import lax
import numpy as np


D_MODEL = 1024
BATCH = 8
SEQ = 2048
DEPTH = 4
DEC_BATCH = 128
DEC_SEQ = 4
PAST_LEN = 16384
PAGE_SIZE = 128

N_MEM = 256
BRANCH_W = D_MODEL // 2
CONV_A_W = 3
CHUNK = 128
GMLP_GROUPS = 4
GMLP_GROUP_W = BRANCH_W // GMLP_GROUPS
CONV_C_W = 31
X_HEADS = 4
X_HEAD_DIM = BRANCH_W // X_HEADS
D_FF = 4 * D_MODEL
N_BRANCH = 4
IN_COLS = 8 * BRANCH_W + N_BRANCH * D_MODEL
ALPHA = (2 * DEPTH) ** 0.25
BETA = (8 * DEPTH) ** -0.25
LN_EPS = 1e-5

kernel_name = "hybrid_gated_conv_gmlp_conformer_memxattn_step"


def layer_norm(x, g, b):
    xf = x.astype(jnp.float32)
    mu = jnp.mean(xf, axis=-1, keepdims=True)
    var = jnp.mean(jnp.square(xf - mu), axis=-1, keepdims=True)
    return ((xf - mu) * lax.rsqrt(var + LN_EPS) * g.astype(jnp.float32) + b.astype(jnp.float32)).astype(x.dtype)


def causal_dwconv(x, buf, w):
    L, C = x.shape[1], x.shape[2]
    xp = jnp.concatenate([buf.astype(x.dtype), x], axis=1)
    y = lax.conv_general_dilated(xp, w[:, None, :].astype(x.dtype), window_strides=(1,), padding='VALID',
                                 dimension_numbers=('NWC', 'WIO', 'NWC'), feature_group_count=C)
    return y, xp[:, L:]


def spatial_gate(v, w_s, b_s):
    Bn, L, _ = v.shape
    T = min(L, CHUNK)
    n = L // T
    vc = v.reshape(Bn, n, T, GMLP_GROUPS, GMLP_GROUP_W)
    mask = jnp.tril(jnp.ones((T, T), dtype=bool))
    ws = jnp.where(mask[None], w_s[:, :T, :T], 0).astype(v.dtype)
    s = jnp.einsum('gts,bnsgc->bntgc', ws, vc) + b_s[:, :T].T[None, None, :, :, None].astype(v.dtype)
    return s.reshape(Bn, L, BRANCH_W)


def mem_attention(q, k, v):
    s = jnp.einsum('blhd,bmhd->bhlm', q, k).astype(jnp.float32) * (X_HEAD_DIM ** -0.5)
    p = jax.nn.softmax(s, axis=-1).astype(v.dtype)
    return jnp.einsum('bhlm,bmhd->blhd', p, v)


def trunk_layer(x, buf_a, buf_c, mem_k, mem_v, w_in, w_conv_a, ln_v_g, ln_v_b, w_s, b_s,
                w_conv_c, b_conv_c, ln_c_g, ln_c_b, w_out_br, w_o, ln1_g, ln1_b,
                w_up, b_up, w_down, b_down, ln2_g, ln2_b):
    Bn, L, _ = x.shape
    W = BRANCH_W
    z = x @ w_in
    xa, ga, gc, zb, zc, q, zg = jnp.split(z, [W, 2 * W, 3 * W, 5 * W, 7 * W, 8 * W], axis=-1)
    conv_a, new_buf_a = causal_dwconv(gc * xa, buf_a, w_conv_a)
    y_a = ga * conv_a
    zb = jax.nn.gelu(zb)
    u, v = zb[..., :W], layer_norm(zb[..., W:], ln_v_g, ln_v_b)
    y_b = u * spatial_gate(v, w_s, b_s)
    glu = zc[..., :W] * jax.nn.sigmoid(zc[..., W:])
    conv_c, new_buf_c = causal_dwconv(glu, buf_c, w_conv_c)
    y_c = jax.nn.silu(layer_norm(conv_c + b_conv_c, ln_c_g, ln_c_b))
    y_x = mem_attention(q.reshape(Bn, L, X_HEADS, X_HEAD_DIM), mem_k, mem_v).reshape(Bn, L, W)
    ys = jnp.stack([y_a, y_b, y_c, y_x], axis=2)
    branch = jnp.einsum('blnw,nwd->blnd', ys, w_out_br)
    gates = jax.nn.sigmoid(zg.reshape(Bn, L, N_BRANCH, D_MODEL))
    mix = jnp.sum(gates * branch, axis=2) @ w_o
    x = layer_norm(ALPHA * x + mix, ln1_g, ln1_b)
    h = jnp.square(jax.nn.relu(x @ w_up + b_up)) @ w_down + b_down
    x = layer_norm(ALPHA * x + h, ln2_g, ln2_b)
    return x, new_buf_a, new_buf_c, v


def setup_inputs(seed: int = 0) -> dict:
    key = jax.random.key(seed)
    ks = jax.random.split(key, 32)
    nrm = lambda k, shape, s: jax.random.normal(k, shape, jnp.float32) * s
    W = BRANCH_W
    return {
        "x_prompt": nrm(ks[0], (BATCH, SEQ, D_MODEL), 1.0),
        "x_sample": nrm(ks[1], (DEC_BATCH, DEC_SEQ, D_MODEL), 1.0),
        "mem_prompt": nrm(ks[2], (BATCH, N_MEM, D_MODEL), 1.0),
        "state_conv_a": nrm(ks[3], (DEPTH, DEC_BATCH, CONV_A_W - 1, W), 1.0),
        "state_conv_c": nrm(ks[4], (DEPTH, DEC_BATCH, CONV_C_W - 1, W), 1.0),
        "cache_mem_k": nrm(ks[5], (DEPTH, DEC_BATCH, N_MEM, X_HEADS, X_HEAD_DIM), 1.0),
        "cache_mem_v": nrm(ks[6], (DEPTH, DEC_BATCH, N_MEM, X_HEADS, X_HEAD_DIM), 1.0),
        "w_in": nrm(ks[7], (DEPTH, D_MODEL, IN_COLS), D_MODEL ** -0.5),
        "w_conv_a": nrm(ks[8], (DEPTH, CONV_A_W, W), CONV_A_W ** -0.5),
        "ln_v_g": 1.0 + nrm(ks[9], (DEPTH, W), 0.02),
        "ln_v_b": nrm(ks[10], (DEPTH, W), 0.02),
        "w_s": nrm(ks[11], (DEPTH, GMLP_GROUPS, CHUNK, CHUNK), 0.5 * CHUNK ** -0.5),
        "b_s": 1.0 + nrm(ks[12], (DEPTH, GMLP_GROUPS, CHUNK), 0.02),
        "w_conv_c": nrm(ks[13], (DEPTH, CONV_C_W, W), CONV_C_W ** -0.5),
        "b_conv_c": nrm(ks[14], (DEPTH, W), 0.02),
        "ln_c_g": 1.0 + nrm(ks[15], (DEPTH, W), 0.02),
        "ln_c_b": nrm(ks[16], (DEPTH, W), 0.02),
        "w_mem_kv": nrm(ks[17], (DEPTH, D_MODEL, 2 * W), D_MODEL ** -0.5),
        "w_out_br": nrm(ks[18], (DEPTH, N_BRANCH, W, D_MODEL), BETA * W ** -0.5),
        "w_o": nrm(ks[19], (DEPTH, D_MODEL, D_MODEL), BETA * D_MODEL ** -0.5),
        "ln1_g": 1.0 + nrm(ks[20], (DEPTH, D_MODEL), 0.02),
        "ln1_b": nrm(ks[21], (DEPTH, D_MODEL), 0.02),
        "w_up": nrm(ks[22], (DEPTH, D_MODEL, D_FF), D_MODEL ** -0.5),
        "b_up": nrm(ks[23], (DEPTH, D_FF), 0.02),
        "w_down": nrm(ks[24], (DEPTH, D_FF, D_MODEL), BETA * D_FF ** -0.5),
        "b_down": nrm(ks[25], (DEPTH, D_MODEL), 0.02),
        "ln2_g": 1.0 + nrm(ks[26], (DEPTH, D_MODEL), 0.02),
        "ln2_b": nrm(ks[27], (DEPTH, D_MODEL), 0.02),
    }


def reference(x_prompt, x_sample, mem_prompt, state_conv_a, state_conv_c, cache_mem_k, cache_mem_v,
              w_in, w_conv_a, ln_v_g, ln_v_b, w_s, b_s, w_conv_c, b_conv_c, ln_c_g, ln_c_b,
              w_mem_kv, w_out_br, w_o, ln1_g, ln1_b, w_up, b_up, w_down, b_down, ln2_g, ln2_b):
    W = BRANCH_W
    Bp = x_prompt.shape[0]
    n_mem = mem_prompt.shape[1]
    zero_a = jnp.zeros((Bp, CONV_A_W - 1, W), x_prompt.dtype)
    zero_c = jnp.zeros((Bp, CONV_C_W - 1, W), x_prompt.dtype)
    yp, ys = x_prompt, x_sample
    pa, pc, pk, pv, sa, sc, sv = [], [], [], [], [], [], []
    for l in range(DEPTH):
        lw = (w_in[l], w_conv_a[l], ln_v_g[l], ln_v_b[l], w_s[l], b_s[l], w_conv_c[l], b_conv_c[l],
              ln_c_g[l], ln_c_b[l], w_out_br[l], w_o[l], ln1_g[l], ln1_b[l], w_up[l], b_up[l],
              w_down[l], b_down[l], ln2_g[l], ln2_b[l])
        kv = jnp.einsum('bmd,de->bme', mem_prompt, w_mem_kv[l])
        mk = kv[..., :W].reshape(Bp, n_mem, X_HEADS, X_HEAD_DIM)
        mv = kv[..., W:].reshape(Bp, n_mem, X_HEADS, X_HEAD_DIM)
        yp, nba, nbc, _ = trunk_layer(yp, zero_a, zero_c, mk, mv, *lw)
        pa.append(nba); pc.append(nbc); pk.append(mk); pv.append(mv)
        ys, sba, sbc, v_new = trunk_layer(ys, state_conv_a[l], state_conv_c[l], cache_mem_k[l], cache_mem_v[l], *lw)
        sa.append(sba); sc.append(sbc); sv.append(v_new)
    return (yp, ys, jnp.stack(pa), jnp.stack(pc), jnp.stack(pk), jnp.stack(pv),
            jnp.stack(sa), jnp.stack(sc), jnp.stack(sv))
```

```python
import functools
import math

import jax
import jax.numpy as jnp
from jax import lax
from jax.experimental import pallas as pl
from jax.experimental.pallas import tpu as pltpu

D_MODEL = 1024
DEPTH = 4
N_MEM = 256
W = D_MODEL // 2
CONV_A_W = 3
CHUNK = 128
GROUPS = 4
GROUP_W = W // GROUPS
CONV_C_W = 31
HEADS = 4
HEAD_DIM = W // HEADS
D_FF = 4 * D_MODEL
N_BRANCH = 4
IN_COLS = 8 * W + N_BRANCH * D_MODEL
ALPHA = (2 * DEPTH) ** 0.25
LN_EPS = 1e-5
ATTN_SCALE = HEAD_DIM ** -0.5

V7X_VMEM_BYTES = 64 * 1024 * 1024
VMEM_LIMIT = V7X_VMEM_BYTES - 8 * 1024 * 1024
SUBLANES = 8

TM_MIX = 256
TM_MLP = 512
HALO_A = 8
HALO_C = 32
CONV_RB = 32
SEQ_BLK = 8

F32 = jnp.float32
BF16 = jnp.bfloat16


def _layer_norm(x, g, b):
    mu = jnp.mean(x, axis=-1, keepdims=True)
    xc = x - mu
    var = jnp.mean(xc * xc, axis=-1, keepdims=True)
    return xc * lax.rsqrt(var + LN_EPS) * g + b


def _dot(a, b):
    return jnp.dot(a, b, preferred_element_type=F32)


def _dot_nt(a, b):
    return lax.dot_general(a, b, (((1,), (1,)), ((), ())), preferred_element_type=F32)


def _softmax_rows(s):
    m = jnp.max(s, axis=-1, keepdims=True)
    e = jnp.exp(s - m)
    return e / jnp.sum(e, axis=-1, keepdims=True)


def _merge_and_norm(x, xb, ys, w_in_ref, wbr_ref, wo_ref, ln1g_ref, ln1b_ref):
    acc = None
    for n in range(N_BRANCH):
        c0 = 8 * W + n * D_MODEL
        gate = jax.nn.sigmoid(_dot(xb, w_in_ref[0, :, c0:c0 + D_MODEL]))
        term = gate * _dot(ys[n].astype(BF16), wbr_ref[0, n])
        acc = term if acc is None else acc + term
    mix = _dot(acc.astype(BF16), wo_ref[0])
    return _layer_norm(ALPHA * x + mix, ln1g_ref[0], ln1b_ref[0])


def _kv_kernel(mem_ref, w_ref, k_ref, v_ref):
    kv = _dot(mem_ref[...].astype(BF16), w_ref[0])
    k_ref[0] = kv[:, :W]
    v_ref[0] = kv[:, W:]


def _kv_project(mem2d, w_kv_bf16):
    rows = mem2d.shape[0]
    return pl.pallas_call(
        _kv_kernel,
        grid=(DEPTH,),
        in_specs=[
            pl.BlockSpec((rows, D_MODEL), lambda l: (0, 0)),
            pl.BlockSpec((1, D_MODEL, 2 * W), lambda l: (l, 0, 0)),
        ],
        out_specs=[
            pl.BlockSpec((1, rows, W), lambda l: (l, 0, 0)),
            pl.BlockSpec((1, rows, W), lambda l: (l, 0, 0)),
        ],
        out_shape=[jax.ShapeDtypeStruct((DEPTH, rows, W), F32)] * 2,
        compiler_params=pltpu.CompilerParams(
            dimension_semantics=("arbitrary",), vmem_limit_bytes=VMEM_LIMIT),
        name="kv_project",
    )(mem2d, w_kv_bf16)


def _prompt_mixer_kernel(x_ref, mk_ref, mv_ref, w_in_ref, wca_ref, lnvg_ref, lnvb_ref, ws_ref,
                         bst_ref, wcc_ref, bcc_ref, lncg_ref, lncb_ref, wbr_ref, wo_ref,
                         ln1g_ref, ln1b_ref,
                         y_ref, nba_ref, nbc_ref, ext_a, ext_c):
    tm = x_ref.shape[1]
    n_chunks = tm // CHUNK

    @pl.when(pl.program_id(1) == 0)
    def _():
        ext_a[0:HALO_A, :] = jnp.zeros((HALO_A, W), F32)
        ext_c[0:HALO_C, :] = jnp.zeros((HALO_C, W), F32)

    x = x_ref[0]
    xb = x.astype(BF16)

    def proj(c0, c1):
        return _dot(xb, w_in_ref[0, :, c0:c1])

    za = proj(0, 3 * W)
    p = za[:, 2 * W:3 * W] * za[:, 0:W]
    ext_a[HALO_A:HALO_A + tm, :] = p
    wca = wca_ref[0]
    conv_a = (wca[0:1, :] * ext_a[HALO_A - 2:HALO_A - 2 + tm, :]
              + wca[1:2, :] * ext_a[HALO_A - 1:HALO_A - 1 + tm, :]
              + wca[2:3, :] * p)
    y_a = za[:, W:2 * W] * conv_a
    nba_ref[0] = ext_a[HALO_A + tm - 2:HALO_A + tm, :]
    ext_a[0:HALO_A, :] = ext_a[tm:tm + HALO_A, :]

    zb = jax.nn.gelu(proj(3 * W, 5 * W))
    u = zb[:, :W]
    v = _layer_norm(zb[:, W:], lnvg_ref[0], lnvb_ref[0]).astype(BF16)
    row = lax.broadcasted_iota(jnp.int32, (CHUNK, CHUNK), 0)
    col = lax.broadcasted_iota(jnp.int32, (CHUNK, CHUNK), 1)
    causal = row >= col
    gate_cols = []
    for g in range(GROUPS):
        wg = jnp.where(causal, ws_ref[0, g], 0.0).astype(BF16)
        rhs = jnp.concatenate(
            [v[n * CHUNK:(n + 1) * CHUNK, g * GROUP_W:(g + 1) * GROUP_W] for n in range(n_chunks)],
            axis=1)
        gate_cols.append(_dot(wg, rhs) + bst_ref[0, :, g:g + 1])
    s = jnp.concatenate(
        [jnp.concatenate([gate_cols[g][:, n * GROUP_W:(n + 1) * GROUP_W] for g in range(GROUPS)], axis=1)
         for n in range(n_chunks)], axis=0)
    y_b = u * s

    zc = proj(5 * W, 7 * W)
    glu = zc[:, :W] * jax.nn.sigmoid(zc[:, W:])
    ext_c[HALO_C:HALO_C + tm, :] = glu
    wcc = wcc_ref[0]
    blocks = []
    for rb in range(tm // CONV_RB):
        base = HALO_C - (CONV_C_W - 1) + rb * CONV_RB
        acc = wcc[0:1, :] * ext_c[base:base + CONV_RB, :]
        for k in range(1, CONV_C_W):
            acc = acc + wcc[k:k + 1, :] * ext_c[base + k:base + k + CONV_RB, :]
        blocks.append(acc)
    conv_c = jnp.concatenate(blocks, axis=0) + bcc_ref[0]
    y_c = jax.nn.silu(_layer_norm(conv_c, lncg_ref[0], lncb_ref[0]))
    nbc_ref[0] = ext_c[HALO_C + tm - (CONV_C_W - 1):HALO_C + tm, :]
    ext_c[0:HALO_C, :] = ext_c[tm:tm + HALO_C, :]

    q = proj(7 * W, 8 * W)
    heads = []
    for h in range(HEADS):
        sl = slice(h * HEAD_DIM, (h + 1) * HEAD_DIM)
        sc = _dot_nt(q[:, sl].astype(BF16), mk_ref[0, 0, :, sl].astype(BF16)) * ATTN_SCALE
        pr = _softmax_rows(sc).astype(BF16)
        heads.append(_dot(pr, mv_ref[0, 0, :, sl].astype(BF16)))
    y_x = jnp.concatenate(heads, axis=1)

    y_ref[0] = _merge_and_norm(x, xb, (y_a, y_b, y_c, y_x), w_in_ref, wbr_ref, wo_ref,
                               ln1g_ref, ln1b_ref)


def _resident(shape_tail, layer):
    nd = len(shape_tail)
    return pl.BlockSpec((1,) + tuple(shape_tail), lambda *_: (layer,) + (0,) * nd,
                        pipeline_mode=pl.Buffered(1))


def _prompt_mixer(layer, x, mk, mv, p):
    bn, seq, _ = x.shape
    tm = TM_MIX
    return pl.pallas_call(
        _prompt_mixer_kernel,
        grid=(bn, seq // tm),
        in_specs=[
            pl.BlockSpec((1, tm, D_MODEL), lambda b, j: (b, j, 0)),
            pl.BlockSpec((1, 1, N_MEM, W), lambda b, j: (layer, b, 0, 0)),
            pl.BlockSpec((1, 1, N_MEM, W), lambda b, j: (layer, b, 0, 0)),
            _resident((D_MODEL, IN_COLS), layer),
            _resident((CONV_A_W, W), layer),
            _resident((1, W), layer),
            _resident((1, W), layer),
            _resident((GROUPS, CHUNK, CHUNK), layer),
            _resident((CHUNK, GROUPS), layer),
            _resident((CONV_C_W, W), layer),
            _resident((1, W), layer),
            _resident((1, W), layer),
            _resident((1, W), layer),
            _resident((N_BRANCH, W, D_MODEL), layer),
            _resident((D_MODEL, D_MODEL), layer),
            _resident((1, D_MODEL), layer),
            _resident((1, D_MODEL), layer),
        ],
        out_specs=[
            pl.BlockSpec((1, tm, D_MODEL), lambda b, j: (b, j, 0)),
            pl.BlockSpec((1, CONV_A_W - 1, W), lambda b, j: (b, 0, 0)),
            pl.BlockSpec((1, CONV_C_W - 1, W), lambda b, j: (b, 0, 0)),
        ],
        out_shape=[
            jax.ShapeDtypeStruct((bn, seq, D_MODEL), F32),
            jax.ShapeDtypeStruct((bn, CONV_A_W - 1, W), F32),
            jax.ShapeDtypeStruct((bn, CONV_C_W - 1, W), F32),
        ],
        scratch_shapes=[
            pltpu.VMEM((tm + HALO_A, W), F32),
            pltpu.VMEM((tm + HALO_C, W), F32),
        ],
        compiler_params=pltpu.CompilerParams(
            dimension_semantics=("arbitrary", "arbitrary"), vmem_limit_bytes=VMEM_LIMIT),
        name="prompt_mixer",
    )(x, mk, mv, p["w_in"], p["w_conv_a"], p["ln_v_g"], p["ln_v_b"], p["w_s"], p["b_s_t"],
      p["w_conv_c"], p["b_conv_c"], p["ln_c_g"], p["ln_c_b"], p["w_out_br"], p["w_o"],
      p["ln1_g"], p["ln1_b"])


def _sample_attn_kernel(x_ref, wq_ref, k_ref, v_ref, o_ref):
    n_pos, sb, _ = x_ref.shape
    rows = n_pos * sb
    xb = x_ref[...].reshape(rows, D_MODEL).astype(BF16)
    q = _dot(xb, wq_ref[0])
    r_seq = lax.broadcasted_iota(jnp.int32, (rows, sb * N_MEM), 0) % sb
    c_seq = lax.broadcasted_iota(jnp.int32, (rows, sb * N_MEM), 1) // N_MEM
    own = r_seq == c_seq
    heads = []
    for h in range(HEADS):
        sl = slice(h * HEAD_DIM, (h + 1) * HEAD_DIM)
        kh = k_ref[:, :, sl].reshape(sb * N_MEM, HEAD_DIM).astype(BF16)
        vh = v_ref[:, :, sl].reshape(sb * N_MEM, HEAD_DIM).astype(BF16)
        sc = _dot_nt(q[:, sl].astype(BF16), kh) * ATTN_SCALE
        sc = jnp.where(own, sc, -jnp.inf)
        pr = _softmax_rows(sc).astype(BF16)
        heads.append(_dot(pr, vh))
    o_ref[...] = jnp.concatenate(heads, axis=1).reshape(n_pos, sb, W)


def _sample_attn(layer, x_t, wq, cache_k, cache_v):
    n_pos, n_seq, _ = x_t.shape
    sb = SEQ_BLK
    return pl.pallas_call(
        _sample_attn_kernel,
        grid=(n_seq // sb,),
        in_specs=[
            pl.BlockSpec((n_pos, sb, D_MODEL), lambda i: (0, i, 0)),
            pl.BlockSpec((1, D_MODEL, W), lambda i: (layer, 0, 0)),
            pl.BlockSpec((None, sb, N_MEM, W), lambda i: (layer, i, 0, 0)),
            pl.BlockSpec((None, sb, N_MEM, W), lambda i: (layer, i, 0, 0)),
        ],
        out_specs=pl.BlockSpec((n_pos, sb, W), lambda i: (0, i, 0)),
        out_shape=jax.ShapeDtypeStruct((n_pos, n_seq, W), F32),
        compiler_params=pltpu.CompilerParams(
            dimension_semantics=("arbitrary",), vmem_limit_bytes=VMEM_LIMIT),
        name="sample_attn",
    )(x_t, wq, cache_k, cache_v)


def _sample_mixer_kernel(x_ref, yx_ref, sa_ref, sc_ref, w_in_ref, wca_ref, lnvg_ref, lnvb_ref,
                         wsr_ref, bsr_ref, wcc_ref, bcc_ref, lncg_ref, lncb_ref, wbr_ref, wo_ref,
                         ln1g_ref, ln1b_ref,
                         y_ref, p_ref, glu_ref, v_ref):
    n_pos, n_seq, _ = x_ref.shape
    rows = n_pos * n_seq
    x = x_ref[...].reshape(rows, D_MODEL)
    xb = x.astype(BF16)

    def proj(c0, c1):
        return _dot(xb, w_in_ref[0, :, c0:c1])

    def pos(a, t):
        return a[t * n_seq:(t + 1) * n_seq, :]

    za = proj(0, 3 * W)
    p = za[:, 2 * W:3 * W] * za[:, 0:W]
    p_ref[...] = p.reshape(n_pos, n_seq, W)
    wca = wca_ref[0]
    hist_a = [sa_ref[0, r] for r in range(CONV_A_W - 1)] + [pos(p, t) for t in range(n_pos)]
    conv_a = jnp.concatenate(
        [sum(wca[k:k + 1, :] * hist_a[t + k] for k in range(CONV_A_W)) for t in range(n_pos)], axis=0)
    y_a = za[:, W:2 * W] * conv_a

    zb = jax.nn.gelu(proj(3 * W, 5 * W))
    u = zb[:, :W]
    v = _layer_norm(zb[:, W:], lnvg_ref[0], lnvb_ref[0])
    v_ref[...] = v.reshape(n_pos, n_seq, W)
    gate_rows = []
    for t in range(n_pos):
        acc = bsr_ref[0, t:t + 1, :]
        for s_ in range(t + 1):
            acc = acc + wsr_ref[0, t * n_pos + s_:t * n_pos + s_ + 1, :] * pos(v, s_)
        gate_rows.append(acc)
    y_b = u * jnp.concatenate(gate_rows, axis=0)

    zc = proj(5 * W, 7 * W)
    glu = zc[:, :W] * jax.nn.sigmoid(zc[:, W:])
    glu_ref[...] = glu.reshape(n_pos, n_seq, W)
    wcc = wcc_ref[0]
    conv_rows = []
    for t in range(n_pos):
        acc = None
        for k in range(CONV_C_W):
            r = t + k
            src = sc_ref[0, r] if r < CONV_C_W - 1 else pos(glu, r - (CONV_C_W - 1))
            term = wcc[k:k + 1, :] * src
            acc = term if acc is None else acc + term
        conv_rows.append(acc)
    conv_c = jnp.concatenate(conv_rows, axis=0) + bcc_ref[0]
    y_c = jax.nn.silu(_layer_norm(conv_c, lncg_ref[0], lncb_ref[0]))

    y_x = yx_ref[...].reshape(rows, W)
    y = _merge_and_norm(x, xb, (y_a, y_b, y_c, y_x), w_in_ref, wbr_ref, wo_ref, ln1g_ref, ln1b_ref)
    y_ref[...] = y.reshape(n_pos, n_seq, D_MODEL)


def _sample_mixer(layer, x_t, yx_t, state_a_t, state_c_t, p):
    n_pos, n_seq, _ = x_t.shape

    def whole(shape):
        nd = len(shape)
        return pl.BlockSpec(tuple(shape), lambda i: (0,) * nd, pipeline_mode=pl.Buffered(1))

    layer_slice = functools.partial(_resident, layer=layer)

    return pl.pallas_call(
        _sample_mixer_kernel,
        grid=(1,),
        in_specs=[
            whole((n_pos, n_seq, D_MODEL)),
            whole((n_pos, n_seq, W)),
            layer_slice((CONV_A_W - 1, n_seq, W)),
            layer_slice((CONV_C_W - 1, n_seq, W)),
            layer_slice((D_MODEL, IN_COLS)),
            layer_slice((CONV_A_W, W)),
            layer_slice((1, W)),
            layer_slice((1, W)),
            layer_slice((n_pos * n_pos, W)),
            layer_slice((n_pos, W)),
            layer_slice((CONV_C_W, W)),
            layer_slice((1, W)),
            layer_slice((1, W)),
            layer_slice((1, W)),
            layer_slice((N_BRANCH, W, D_MODEL)),
            layer_slice((D_MODEL, D_MODEL)),
            layer_slice((1, D_MODEL)),
            layer_slice((1, D_MODEL)),
        ],
        out_specs=[
            whole((n_pos, n_seq, D_MODEL)),
            whole((n_pos, n_seq, W)),
            whole((n_pos, n_seq, W)),
            whole((n_pos, n_seq, W)),
        ],
        out_shape=[
            jax.ShapeDtypeStruct((n_pos, n_seq, D_MODEL), F32),
            jax.ShapeDtypeStruct((n_pos, n_seq, W), F32),
            jax.ShapeDtypeStruct((n_pos, n_seq, W), F32),
            jax.ShapeDtypeStruct((n_pos, n_seq, W), F32),
        ],
        compiler_params=pltpu.CompilerParams(
            dimension_semantics=("arbitrary",), vmem_limit_bytes=VMEM_LIMIT),
        name="sample_mixer",
    )(x_t, yx_t, state_a_t, state_c_t, p["w_in"], p["w_conv_a"], p["ln_v_g"], p["ln_v_b"],
      p["w_s_rows"], p["b_s_rows"], p["w_conv_c"], p["b_conv_c"], p["ln_c_g"], p["ln_c_b"],
      p["w_out_br"], p["w_o"], p["ln1_g"], p["ln1_b"])


def _mlp_kernel(x_ref, wu_ref, bu_ref, wd_ref, bd_ref, g_ref, b_ref, y_ref):
    x = x_ref[...]
    h = jnp.maximum(_dot(x.astype(BF16), wu_ref[0]) + bu_ref[0], 0.0)
    h = _dot((h * h).astype(BF16), wd_ref[0]) + bd_ref[0]
    y_ref[...] = _layer_norm(ALPHA * x + h, g_ref[0], b_ref[0])


def _mlp(layer, x2d, p):
    rows = x2d.shape[0]
    tm = min(TM_MLP, rows)
    return pl.pallas_call(
        _mlp_kernel,
        grid=(rows // tm,),
        in_specs=[
            pl.BlockSpec((tm, D_MODEL), lambda i: (i, 0)),
            _resident((D_MODEL, D_FF), layer),
            _resident((1, D_FF), layer),
            _resident((D_FF, D_MODEL), layer),
            _resident((1, D_MODEL), layer),
            _resident((1, D_MODEL), layer),
            _resident((1, D_MODEL), layer),
        ],
        out_specs=pl.BlockSpec((tm, D_MODEL), lambda i: (i, 0)),
        out_shape=jax.ShapeDtypeStruct((rows, D_MODEL), F32),
        compiler_params=pltpu.CompilerParams(
            dimension_semantics=("arbitrary",), vmem_limit_bytes=VMEM_LIMIT),
        name="mlp",
    )(x2d, p["w_up"], p["b_up"], p["w_down"], p["b_down"], p["ln2_g"], p["ln2_b"])


def kernel(x_prompt, x_sample, mem_prompt, state_conv_a, state_conv_c, cache_mem_k, cache_mem_v,
           w_in, w_conv_a, ln_v_g, ln_v_b, w_s, b_s, w_conv_c, b_conv_c, ln_c_g, ln_c_b,
           w_mem_kv, w_out_br, w_o, ln1_g, ln1_b, w_up, b_up, w_down, b_down, ln2_g, ln2_b):
    bp, seq, _ = x_prompt.shape
    n_seq, n_pos, _ = x_sample.shape
    assert seq % TM_MIX == 0 and TM_MIX % CHUNK == 0 and TM_MIX % CONV_RB == 0
    assert n_seq % SEQ_BLK == 0 and n_pos <= CHUNK and n_pos >= CONV_A_W - 1

    row = lambda a: a[:, None, :]
    w_in_bf = w_in.astype(BF16)
    params = {
        "w_in": w_in_bf,
        "w_conv_a": w_conv_a, "ln_v_g": row(ln_v_g), "ln_v_b": row(ln_v_b),
        "w_s": w_s, "b_s_t": jnp.swapaxes(b_s, 1, 2),
        "w_s_rows": jnp.repeat(
            jnp.transpose(w_s[:, :, :n_pos, :n_pos], (0, 2, 3, 1)).reshape(DEPTH, n_pos * n_pos, GROUPS),
            GROUP_W, axis=-1),
        "b_s_rows": jnp.repeat(jnp.swapaxes(b_s[:, :, :n_pos], 1, 2), GROUP_W, axis=-1),
        "w_conv_c": w_conv_c, "b_conv_c": row(b_conv_c), "ln_c_g": row(ln_c_g), "ln_c_b": row(ln_c_b),
        "w_out_br": w_out_br.astype(BF16), "w_o": w_o.astype(BF16),
        "ln1_g": row(ln1_g), "ln1_b": row(ln1_b),
        "w_up": w_up.astype(BF16), "b_up": row(b_up), "w_down": w_down.astype(BF16),
        "b_down": row(b_down), "ln2_g": row(ln2_g), "ln2_b": row(ln2_b),
    }
    w_q = w_in_bf[:, :, 7 * W:8 * W]

    pk, pv = _kv_project(mem_prompt.reshape(bp * N_MEM, D_MODEL), w_mem_kv.astype(BF16))
    pk = pk.reshape(DEPTH, bp, N_MEM, W)
    pv = pv.reshape(DEPTH, bp, N_MEM, W)

    ys = jnp.swapaxes(x_sample, 0, 1)
    state_a_t = jnp.swapaxes(state_conv_a, 1, 2)
    state_c_t = jnp.swapaxes(state_conv_c, 1, 2)
    cache_k = cache_mem_k.reshape(DEPTH, n_seq, N_MEM, W)
    cache_v = cache_mem_v.reshape(DEPTH, n_seq, N_MEM, W)

    yp = x_prompt
    pa, pc, sa, sc, sv = [], [], [], [], []
    for l in range(DEPTH):
        x1, nba, nbc = _prompt_mixer(l, yp, pk, pv, params)
        yp = _mlp(l, x1.reshape(bp * seq, D_MODEL), params).reshape(bp, seq, D_MODEL)
        pa.append(nba)
        pc.append(nbc)

        yx = _sample_attn(l, ys, w_q, cache_k, cache_v)
        s1, p_t, glu_t, v_t = _sample_mixer(l, ys, yx, state_a_t, state_c_t, params)
        ys = _mlp(l, s1.reshape(n_pos * n_seq, D_MODEL), params).reshape(n_pos, n_seq, D_MODEL)
        sa.append(jnp.swapaxes(p_t[n_pos - (CONV_A_W - 1):], 0, 1))
        sc.append(jnp.concatenate([state_conv_c[l][:, n_pos:], jnp.swapaxes(glu_t, 0, 1)], axis=1))
        sv.append(jnp.swapaxes(v_t, 0, 1))

    return (yp, jnp.swapaxes(ys, 0, 1), jnp.stack(pa), jnp.stack(pc),
            pk.reshape(DEPTH, bp, N_MEM, HEADS, HEAD_DIM), pv.reshape(DEPTH, bp, N_MEM, HEADS, HEAD_DIM),
            jnp.stack(sa), jnp.stack(sc), jnp.stack(sv))
```

```python
import functools
import math

import jax
import jax.numpy as jnp
from jax import lax
from jax.experimental import pallas as pl
from jax.experimental.pallas import tpu as pltpu

D_MODEL = 1024
DEPTH = 4
N_MEM = 256
W = D_MODEL // 2
CONV_A_W = 3
CHUNK = 128
GROUPS = 4
GROUP_W = W // GROUPS
CONV_C_W = 31
HEADS = 4
HEAD_DIM = W // HEADS
D_FF = 4 * D_MODEL
N_BRANCH = 4
IN_COLS = 8 * W + N_BRANCH * D_MODEL
ALPHA = (2 * DEPTH) ** 0.25
LN_EPS = 1e-5
ATTN_SCALE = HEAD_DIM ** -0.5

V7X_VMEM_BYTES = 64 * 1024 * 1024
VMEM_LIMIT = V7X_VMEM_BYTES - 8 * 1024 * 1024
SUBLANES = 8
LANES = 128

TM_MIX = 256
TM_MLP = 512
HALO_A = 8
HALO_C = 32
CONV_RB = 64
SEQ_BLK = 8

F32 = jnp.float32
BF16 = jnp.bfloat16


def _layer_norm(x, g, b):
    mu = jnp.mean(x, axis=-1, keepdims=True)
    xc = x - mu
    var = jnp.mean(xc * xc, axis=-1, keepdims=True)
    return xc * lax.rsqrt(var + LN_EPS) * g + b


def _dot(a, b):
    return jnp.dot(a, b, preferred_element_type=F32)


def _dot_nt(a, b):
    return lax.dot_general(a, b, (((1,), (1,)), ((), ())), preferred_element_type=F32)


def _softmax_rows(s):
    m = jnp.max(s, axis=-1, keepdims=True)
    e = jnp.exp(s - m)
    return e / jnp.sum(e, axis=-1, keepdims=True)


def _merge_and_norm(x, gate_pre, ys, wbr_ref, wo_ref, ln1g_ref, ln1b_ref):
    acc = None
    for n in range(N_BRANCH):
        term = jax.nn.sigmoid(gate_pre[n]) * _dot(ys[n].astype(BF16), wbr_ref[0, n])
        acc = term if acc is None else acc + term
    mix = _dot(acc.astype(BF16), wo_ref[0])
    return _layer_norm(ALPHA * x + mix, ln1g_ref[0], ln1b_ref[0])


def _kv_kernel(mem_ref, w_ref, k_ref, v_ref):
    kv = _dot(mem_ref[...].astype(BF16), w_ref[0])
    k_ref[0] = kv[:, :W]
    v_ref[0] = kv[:, W:]


def _kv_project(mem2d, w_kv_bf16):
    rows = mem2d.shape[0]
    return pl.pallas_call(
        _kv_kernel,
        grid=(DEPTH,),
        in_specs=[
            pl.BlockSpec((rows, D_MODEL), lambda l: (0, 0)),
            pl.BlockSpec((1, D_MODEL, 2 * W), lambda l: (l, 0, 0)),
        ],
        out_specs=[
            pl.BlockSpec((1, rows, W), lambda l: (l, 0, 0)),
            pl.BlockSpec((1, rows, W), lambda l: (l, 0, 0)),
        ],
        out_shape=[jax.ShapeDtypeStruct((DEPTH, rows, W), F32)] * 2,
        compiler_params=pltpu.CompilerParams(
            dimension_semantics=("arbitrary",), vmem_limit_bytes=VMEM_LIMIT),
        name="kv_project",
    )(mem2d, w_kv_bf16)


def _conv_c_rows(ext_c, wb_ref, r0):
    lead = HALO_C - (CONV_C_W - 1)
    span = CONV_RB + HALO_C
    cols = []
    for c in range(W // LANES):
        lanes = slice(c * LANES, (c + 1) * LANES)
        win = ext_c[r0:r0 + span, lanes]
        acc = None
        for b in range(SUBLANES):
            sh = win if b == 0 else pltpu.roll(win, span - b, axis=0)
            for a in range(HALO_C // SUBLANES + 1):
                k = a * SUBLANES + b - lead
                if 0 <= k < CONV_C_W:
                    tap = sh[a * SUBLANES:a * SUBLANES + CONV_RB].reshape(CONV_RB // SUBLANES, SUBLANES, LANES)
                    term = wb_ref[k * SUBLANES:(k + 1) * SUBLANES, lanes][None] * tap
                    acc = term if acc is None else acc + term
        cols.append(acc.reshape(CONV_RB, LANES))
    return jnp.concatenate(cols, axis=1)


def _prompt_mixer_kernel(x_ref, mk_ref, mv_ref, w_in_ref, wca_ref, lnvg_ref, lnvb_ref, ws_ref,
                         bst_ref, wcc_ref, bcc_ref, lncg_ref, lncb_ref, wbr_ref, wo_ref,
                         ln1g_ref, ln1b_ref,
                         y_ref, nba_ref, nbc_ref, ext_a, ext_c, wb_c):
    tm = x_ref.shape[1]
    n_chunks = tm // CHUNK

    @pl.when(pl.program_id(1) == 0)
    def _():
        ext_a[0:HALO_A, :] = jnp.zeros((HALO_A, W), F32)
        ext_c[0:HALO_C, :] = jnp.zeros((HALO_C, W), F32)
        for k in range(CONV_C_W):
            wb_c[k * SUBLANES:(k + 1) * SUBLANES, :] = jnp.broadcast_to(wcc_ref[0, k:k + 1, :], (SUBLANES, W))

    x = x_ref[0]
    xb = x.astype(BF16)

    def proj(c0, c1):
        return _dot(xb, w_in_ref[0, :, c0:c1])

    zc = proj(5 * W, 7 * W)
    glu = zc[:, :W] * jax.nn.sigmoid(zc[:, W:])
    ext_c[HALO_C:HALO_C + tm, :] = glu
    gate_pre = []
    blocks = []
    for rb in range(tm // CONV_RB):
        if len(gate_pre) < N_BRANCH:
            c0 = 8 * W + len(gate_pre) * D_MODEL
            gate_pre.append(proj(c0, c0 + D_MODEL))
        blocks.append(_conv_c_rows(ext_c, wb_c, rb * CONV_RB))
    while len(gate_pre) < N_BRANCH:
        c0 = 8 * W + len(gate_pre) * D_MODEL
        gate_pre.append(proj(c0, c0 + D_MODEL))
    conv_c = jnp.concatenate(blocks, axis=0) + bcc_ref[0]
    nbc_ref[0] = ext_c[HALO_C + tm - (CONV_C_W - 1):HALO_C + tm, :]
    ext_c[0:HALO_C, :] = ext_c[tm:tm + HALO_C, :]

    za = proj(0, 3 * W)
    y_c = jax.nn.silu(_layer_norm(conv_c, lncg_ref[0], lncb_ref[0]))
    p = za[:, 2 * W:3 * W] * za[:, 0:W]
    ext_a[HALO_A:HALO_A + tm, :] = p
    wca = wca_ref[0]
    conv_a = (wca[0:1, :] * ext_a[HALO_A - 2:HALO_A - 2 + tm, :]
              + wca[1:2, :] * ext_a[HALO_A - 1:HALO_A - 1 + tm, :]
              + wca[2:3, :] * p)
    y_a = za[:, W:2 * W] * conv_a
    nba_ref[0] = ext_a[HALO_A + tm - 2:HALO_A + tm, :]
    ext_a[0:HALO_A, :] = ext_a[tm:tm + HALO_A, :]

    zb = jax.nn.gelu(proj(3 * W, 5 * W))
    q = proj(7 * W, 8 * W)
    u = zb[:, :W]
    v = _layer_norm(zb[:, W:], lnvg_ref[0], lnvb_ref[0]).astype(BF16)
    row = lax.broadcasted_iota(jnp.int32, (CHUNK, CHUNK), 0)
    col = lax.broadcasted_iota(jnp.int32, (CHUNK, CHUNK), 1)
    causal = row >= col
    gate_cols = []
    for g in range(GROUPS):
        wg = jnp.where(causal, ws_ref[0, g], 0.0).astype(BF16)
        rhs = jnp.concatenate(
            [v[n * CHUNK:(n + 1) * CHUNK, g * GROUP_W:(g + 1) * GROUP_W] for n in range(n_chunks)],
            axis=1)
        gate_cols.append(_dot(wg, rhs) + bst_ref[0, :, g:g + 1])
    s = jnp.concatenate(
        [jnp.concatenate([gate_cols[g][:, n * GROUP_W:(n + 1) * GROUP_W] for g in range(GROUPS)], axis=1)
         for n in range(n_chunks)], axis=0)
    y_b = u * s

    heads = []
    for h in range(HEADS):
        sl = slice(h * HEAD_DIM, (h + 1) * HEAD_DIM)
        sc = _dot_nt(q[:, sl].astype(BF16), mk_ref[0, 0, :, sl].astype(BF16)) * ATTN_SCALE
        pr = _softmax_rows(sc).astype(BF16)
        heads.append(_dot(pr, mv_ref[0, 0, :, sl].astype(BF16)))
    y_x = jnp.concatenate(heads, axis=1)

    y_ref[0] = _merge_and_norm(x, gate_pre, (y_a, y_b, y_c, y_x), wbr_ref, wo_ref, ln1g_ref, ln1b_ref)


def _resident(shape_tail, layer):
    nd = len(shape_tail)
    return pl.BlockSpec((1,) + tuple(shape_tail), lambda *_: (layer,) + (0,) * nd,
                        pipeline_mode=pl.Buffered(1))


def _prompt_mixer(layer, x, mk, mv, p):
    bn, seq, _ = x.shape
    tm = TM_MIX
    return pl.pallas_call(
        _prompt_mixer_kernel,
        grid=(bn, seq // tm),
        in_specs=[
            pl.BlockSpec((1, tm, D_MODEL), lambda b, j: (b, j, 0)),
            pl.BlockSpec((1, 1, N_MEM, W), lambda b, j: (layer, b, 0, 0)),
            pl.BlockSpec((1, 1, N_MEM, W), lambda b, j: (layer, b, 0, 0)),
            _resident((D_MODEL, IN_COLS), layer),
            _resident((CONV_A_W, W), layer),
            _resident((1, W), layer),
            _resident((1, W), layer),
            _resident((GROUPS, CHUNK, CHUNK), layer),
            _resident((CHUNK, GROUPS), layer),
            _resident((CONV_C_W, W), layer),
            _resident((1, W), layer),
            _resident((1, W), layer),
            _resident((1, W), layer),
            _resident((N_BRANCH, W, D_MODEL), layer),
            _resident((D_MODEL, D_MODEL), layer),
            _resident((1, D_MODEL), layer),
            _resident((1, D_MODEL), layer),
        ],
        out_specs=[
            pl.BlockSpec((1, tm, D_MODEL), lambda b, j: (b, j, 0)),
            pl.BlockSpec((1, CONV_A_W - 1, W), lambda b, j: (b, 0, 0)),
            pl.BlockSpec((1, CONV_C_W - 1, W), lambda b, j: (b, 0, 0)),
        ],
        out_shape=[
            jax.ShapeDtypeStruct((bn, seq, D_MODEL), F32),
            jax.ShapeDtypeStruct((bn, CONV_A_W - 1, W), F32),
            jax.ShapeDtypeStruct((bn, CONV_C_W - 1, W), F32),
        ],
        scratch_shapes=[
            pltpu.VMEM((tm + HALO_A, W), F32),
            pltpu.VMEM((tm + HALO_C, W), F32),
            pltpu.VMEM((CONV_C_W * SUBLANES, W), F32),
        ],
        compiler_params=pltpu.CompilerParams(
            dimension_semantics=("arbitrary", "arbitrary"), vmem_limit_bytes=VMEM_LIMIT),
        name="prompt_mixer",
    )(x, mk, mv, p["w_in"], p["w_conv_a"], p["ln_v_g"], p["ln_v_b"], p["w_s"], p["b_s_t"],
      p["w_conv_c"], p["b_conv_c"], p["ln_c_g"], p["ln_c_b"], p["w_out_br"], p["w_o"],
      p["ln1_g"], p["ln1_b"])


def _sample_attn_kernel(x_ref, wq_ref, k_ref, v_ref, o_ref):
    n_pos, sb, _ = x_ref.shape
    rows = n_pos * sb
    xb = x_ref[...].reshape(rows, D_MODEL).astype(BF16)
    q = _dot(xb, wq_ref[0])
    r_seq = lax.broadcasted_iota(jnp.int32, (rows, sb * N_MEM), 0) % sb
    c_seq = lax.broadcasted_iota(jnp.int32, (rows, sb * N_MEM), 1) // N_MEM
    own = r_seq == c_seq
    heads = []
    for h in range(HEADS):
        sl = slice(h * HEAD_DIM, (h + 1) * HEAD_DIM)
        kh = k_ref[:, pl.ds(h, N_MEM, stride=HEADS), :].reshape(sb * N_MEM, HEAD_DIM).astype(BF16)
        vh = v_ref[:, pl.ds(h, N_MEM, stride=HEADS), :].reshape(sb * N_MEM, HEAD_DIM).astype(BF16)
        sc = _dot_nt(q[:, sl].astype(BF16), kh) * ATTN_SCALE
        sc = jnp.where(own, sc, -jnp.inf)
        pr = _softmax_rows(sc).astype(BF16)
        heads.append(_dot(pr, vh))
    o_ref[...] = jnp.concatenate(heads, axis=1).reshape(n_pos, sb, W)


def _sample_attn(layer, x_t, wq, cache_k, cache_v):
    n_pos, n_seq, _ = x_t.shape
    sb = SEQ_BLK
    return pl.pallas_call(
        _sample_attn_kernel,
        grid=(n_seq // sb,),
        in_specs=[
            pl.BlockSpec((n_pos, sb, D_MODEL), lambda i: (0, i, 0)),
            pl.BlockSpec((1, D_MODEL, W), lambda i: (layer, 0, 0)),
            pl.BlockSpec((None, sb, N_MEM * HEADS, HEAD_DIM), lambda i: (layer, i, 0, 0)),
            pl.BlockSpec((None, sb, N_MEM * HEADS, HEAD_DIM), lambda i: (layer, i, 0, 0)),
        ],
        out_specs=pl.BlockSpec((n_pos, sb, W), lambda i: (0, i, 0)),
        out_shape=jax.ShapeDtypeStruct((n_pos, n_seq, W), F32),
        compiler_params=pltpu.CompilerParams(
            dimension_semantics=("arbitrary",), vmem_limit_bytes=VMEM_LIMIT),
        name="sample_attn",
    )(x_t, wq, cache_k, cache_v)


def _sample_mixer_kernel(x_ref, yx_ref, sa_ref, sc_ref, w_in_ref, wca_ref, lnvg_ref, lnvb_ref,
                         wsr_ref, bsr_ref, wcc_ref, bcc_ref, lncg_ref, lncb_ref, wbr_ref, wo_ref,
                         ln1g_ref, ln1b_ref,
                         y_ref, p_ref, glu_ref, v_ref):
    n_pos, n_seq, _ = x_ref.shape
    rows = n_pos * n_seq
    x = x_ref[...].reshape(rows, D_MODEL)
    xb = x.astype(BF16)

    def proj(c0, c1):
        return _dot(xb, w_in_ref[0, :, c0:c1])

    def pos(a, t):
        return a[t * n_seq:(t + 1) * n_seq, :]

    za = proj(0, 3 * W)
    p = za[:, 2 * W:3 * W] * za[:, 0:W]
    p_ref[...] = p.reshape(n_pos, n_seq, W)
    wca = wca_ref[0]
    hist_a = [sa_ref[0, r] for r in range(CONV_A_W - 1)] + [pos(p, t) for t in range(n_pos)]
    conv_a = jnp.concatenate(
        [sum(wca[k:k + 1, :] * hist_a[t + k] for k in range(CONV_A_W)) for t in range(n_pos)], axis=0)
    y_a = za[:, W:2 * W] * conv_a

    zb = jax.nn.gelu(proj(3 * W, 5 * W))
    u = zb[:, :W]
    v = _layer_norm(zb[:, W:], lnvg_ref[0], lnvb_ref[0])
    v_ref[...] = v.reshape(n_pos, n_seq, W)
    gate_rows = []
    for t in range(n_pos):
        acc = bsr_ref[0, t:t + 1, :]
        for s_ in range(t + 1):
            acc = acc + wsr_ref[0, t * n_pos + s_:t * n_pos + s_ + 1, :] * pos(v, s_)
        gate_rows.append(acc)
    y_b = u * jnp.concatenate(gate_rows, axis=0)

    zc = proj(5 * W, 7 * W)
    glu = zc[:, :W] * jax.nn.sigmoid(zc[:, W:])
    glu_ref[...] = glu.reshape(n_pos, n_seq, W)
    wcc = wcc_ref[0]
    conv_rows = []
    for t in range(n_pos):
        acc = None
        for k in range(CONV_C_W):
            r = t + k
            src = sc_ref[0, r] if r < CONV_C_W - 1 else pos(glu, r - (CONV_C_W - 1))
            term = wcc[k:k + 1, :] * src
            acc = term if acc is None else acc + term
        conv_rows.append(acc)
    conv_c = jnp.concatenate(conv_rows, axis=0) + bcc_ref[0]
    y_c = jax.nn.silu(_layer_norm(conv_c, lncg_ref[0], lncb_ref[0]))

    y_x = yx_ref[...].reshape(rows, W)
    gate_pre = [proj(8 * W + n * D_MODEL, 8 * W + (n + 1) * D_MODEL) for n in range(N_BRANCH)]
    y = _merge_and_norm(x, gate_pre, (y_a, y_b, y_c, y_x), wbr_ref, wo_ref, ln1g_ref, ln1b_ref)
    y_ref[...] = y.reshape(n_pos, n_seq, D_MODEL)


def _sample_mixer(layer, x_t, yx_t, state_a_t, state_c_t, p):
    n_pos, n_seq, _ = x_t.shape

    def whole(shape):
        nd = len(shape)
        return pl.BlockSpec(tuple(shape), lambda i: (0,) * nd, pipeline_mode=pl.Buffered(1))

    layer_slice = functools.partial(_resident, layer=layer)

    return pl.pallas_call(
        _sample_mixer_kernel,
        grid=(1,),
        in_specs=[
            whole((n_pos, n_seq, D_MODEL)),
            whole((n_pos, n_seq, W)),
            layer_slice((CONV_A_W - 1, n_seq, W)),
            layer_slice((CONV_C_W - 1, n_seq, W)),
            layer_slice((D_MODEL, IN_COLS)),
            layer_slice((CONV_A_W, W)),
            layer_slice((1, W)),
            layer_slice((1, W)),
            layer_slice((n_pos * n_pos, W)),
            layer_slice((n_pos, W)),
            layer_slice((CONV_C_W, W)),
            layer_slice((1, W)),
            layer_slice((1, W)),
            layer_slice((1, W)),
            layer_slice((N_BRANCH, W, D_MODEL)),
            layer_slice((D_MODEL, D_MODEL)),
            layer_slice((1, D_MODEL)),
            layer_slice((1, D_MODEL)),
        ],
        out_specs=[
            whole((n_pos, n_seq, D_MODEL)),
            whole((n_pos, n_seq, W)),
            whole((n_pos, n_seq, W)),
            whole((n_pos, n_seq, W)),
        ],
        out_shape=[
            jax.ShapeDtypeStruct((n_pos, n_seq, D_MODEL), F32),
            jax.ShapeDtypeStruct((n_pos, n_seq, W), F32),
            jax.ShapeDtypeStruct((n_pos, n_seq, W), F32),
            jax.ShapeDtypeStruct((n_pos, n_seq, W), F32),
        ],
        compiler_params=pltpu.CompilerParams(
            dimension_semantics=("arbitrary",), vmem_limit_bytes=VMEM_LIMIT),
        name="sample_mixer",
    )(x_t, yx_t, state_a_t, state_c_t, p["w_in"], p["w_conv_a"], p["ln_v_g"], p["ln_v_b"],
      p["w_s_rows"], p["b_s_rows"], p["w_conv_c"], p["b_conv_c"], p["ln_c_g"], p["ln_c_b"],
      p["w_out_br"], p["w_o"], p["ln1_g"], p["ln1_b"])


def _mlp_kernel(x_ref, wu_ref, bu_ref, wd_ref, bd_ref, g_ref, b_ref, y_ref):
    x = x_ref[...]
    h = jnp.maximum(_dot(x.astype(BF16), wu_ref[0]) + bu_ref[0], 0.0)
    h = _dot((h * h).astype(BF16), wd_ref[0]) + bd_ref[0]
    y_ref[...] = _layer_norm(ALPHA * x + h, g_ref[0], b_ref[0])


def _mlp(layer, x2d, p):
    rows = x2d.shape[0]
    tm = min(TM_MLP, rows)
    return pl.pallas_call(
        _mlp_kernel,
        grid=(rows // tm,),
        in_specs=[
            pl.BlockSpec((tm, D_MODEL), lambda i: (i, 0)),
            _resident((D_MODEL, D_FF), layer),
            _resident((1, D_FF), layer),
            _resident((D_FF, D_MODEL), layer),
            _resident((1, D_MODEL), layer),
            _resident((1, D_MODEL), layer),
            _resident((1, D_MODEL), layer),
        ],
        out_specs=pl.BlockSpec((tm, D_MODEL), lambda i: (i, 0)),
        out_shape=jax.ShapeDtypeStruct((rows, D_MODEL), F32),
        compiler_params=pltpu.CompilerParams(
            dimension_semantics=("arbitrary",), vmem_limit_bytes=VMEM_LIMIT),
        name="mlp",
    )(x2d, p["w_up"], p["b_up"], p["w_down"], p["b_down"], p["ln2_g"], p["ln2_b"])


def kernel(x_prompt, x_sample, mem_prompt, state_conv_a, state_conv_c, cache_mem_k, cache_mem_v,
           w_in, w_conv_a, ln_v_g, ln_v_b, w_s, b_s, w_conv_c, b_conv_c, ln_c_g, ln_c_b,
           w_mem_kv, w_out_br, w_o, ln1_g, ln1_b, w_up, b_up, w_down, b_down, ln2_g, ln2_b):
    bp, seq, _ = x_prompt.shape
    n_seq, n_pos, _ = x_sample.shape
    assert seq % TM_MIX == 0 and TM_MIX % CHUNK == 0 and TM_MIX % CONV_RB == 0
    assert n_seq % SEQ_BLK == 0 and n_pos <= CHUNK and n_pos >= CONV_A_W - 1

    row = lambda a: a[:, None, :]
    w_in_bf = w_in.astype(BF16)
    params = {
        "w_in": w_in_bf,
        "w_conv_a": w_conv_a, "ln_v_g": row(ln_v_g), "ln_v_b": row(ln_v_b),
        "w_s": w_s, "b_s_t": jnp.swapaxes(b_s, 1, 2),
        "w_s_rows": jnp.repeat(
            jnp.transpose(w_s[:, :, :n_pos, :n_pos], (0, 2, 3, 1)).reshape(DEPTH, n_pos * n_pos, GROUPS),
            GROUP_W, axis=-1),
        "b_s_rows": jnp.repeat(jnp.swapaxes(b_s[:, :, :n_pos], 1, 2), GROUP_W, axis=-1),
        "w_conv_c": w_conv_c, "b_conv_c": row(b_conv_c), "ln_c_g": row(ln_c_g), "ln_c_b": row(ln_c_b),
        "w_out_br": w_out_br.astype(BF16), "w_o": w_o.astype(BF16),
        "ln1_g": row(ln1_g), "ln1_b": row(ln1_b),
        "w_up": w_up.astype(BF16), "b_up": row(b_up), "w_down": w_down.astype(BF16),
        "b_down": row(b_down), "ln2_g": row(ln2_g), "ln2_b": row(ln2_b),
    }
    w_q = w_in_bf[:, :, 7 * W:8 * W]

    pk, pv = _kv_project(mem_prompt.reshape(bp * N_MEM, D_MODEL), w_mem_kv.astype(BF16))
    pk = pk.reshape(DEPTH, bp, N_MEM, W)
    pv = pv.reshape(DEPTH, bp, N_MEM, W)

    ys = jnp.swapaxes(x_sample, 0, 1)
    state_a_t = jnp.swapaxes(state_conv_a, 1, 2)
    state_c_t = jnp.swapaxes(state_conv_c, 1, 2)
    cache_k = cache_mem_k.reshape(DEPTH, n_seq, N_MEM * HEADS, HEAD_DIM)
    cache_v = cache_mem_v.reshape(DEPTH, n_seq, N_MEM * HEADS, HEAD_DIM)

    yp = x_prompt
    pa, pc, sa, sc, sv = [], [], [], [], []
    for l in range(DEPTH):
        x1, nba, nbc = _prompt_mixer(l, yp, pk, pv, params)
        yp = _mlp(l, x1.reshape(bp * seq, D_MODEL), params).reshape(bp, seq, D_MODEL)
        pa.append(nba)
        pc.append(nbc)

        yx = _sample_attn(l, ys, w_q, cache_k, cache_v)
        s1, p_t, glu_t, v_t = _sample_mixer(l, ys, yx, state_a_t, state_c_t, params)
        ys = _mlp(l, s1.reshape(n_pos * n_seq, D_MODEL), params).reshape(n_pos, n_seq, D_MODEL)
        sa.append(jnp.swapaxes(p_t[n_pos - (CONV_A_W - 1):], 0, 1))
        sc.append(jnp.concatenate([state_conv_c[l][:, n_pos:], jnp.swapaxes(glu_t, 0, 1)], axis=1))
        sv.append(jnp.swapaxes(v_t, 0, 1))

    return (yp, jnp.swapaxes(ys, 0, 1), jnp.stack(pa), jnp.stack(pc),
            pk.reshape(DEPTH, bp, N_MEM, HEADS, HEAD_DIM), pv.reshape(DEPTH, bp, N_MEM, HEADS, HEAD_DIM),
            jnp.stack(sa), jnp.stack(sc), jnp.stack(sv))
```

```python
import functools
import math

import jax
import jax.numpy as jnp
from jax import lax
from jax.experimental import pallas as pl
from jax.experimental.pallas import tpu as pltpu

D_MODEL = 1024
DEPTH = 4
N_MEM = 256
W = D_MODEL // 2
CONV_A_W = 3
CHUNK = 128
GROUPS = 4
GROUP_W = W // GROUPS
CONV_C_W = 31
HEADS = 4
HEAD_DIM = W // HEADS
D_FF = 4 * D_MODEL
N_BRANCH = 4
IN_COLS = 8 * W + N_BRANCH * D_MODEL
ALPHA = (2 * DEPTH) ** 0.25
LN_EPS = 1e-5
ATTN_SCALE = HEAD_DIM ** -0.5

V7X_VMEM_BYTES = 64 * 1024 * 1024
VMEM_LIMIT = V7X_VMEM_BYTES - 8 * 1024 * 1024
SUBLANES = 8
LANES = 128

TM_MIX = 256
TM_MLP = 512
HALO_A = 8
HALO_C = 32
CONV_RB = 64
PROJ_COLS = 1024
SEQ_BLK = 8
SEQ_BLK_MIX = 32

F32 = jnp.float32
BF16 = jnp.bfloat16


def _layer_norm(x, g, b):
    mu = jnp.mean(x, axis=-1, keepdims=True)
    xc = x - mu
    var = jnp.mean(xc * xc, axis=-1, keepdims=True)
    return xc * lax.rsqrt(var + LN_EPS) * g + b


def _dot(a, b):
    return jnp.dot(a, b, preferred_element_type=F32)


def _dot_nt(a, b):
    return lax.dot_general(a, b, (((1,), (1,)), ((), ())), preferred_element_type=F32)


def _gelu_tanh(x):
    c1 = math.sqrt(2.0 / math.pi)
    u = x * (c1 + (c1 * 0.044715) * (x * x))
    h = 0.5 * x
    return h + h * jnp.tanh(u)


def _softmax_rows(s):
    m = jnp.max(s, axis=-1, keepdims=True)
    e = jnp.exp(s - m)
    return e / jnp.sum(e, axis=-1, keepdims=True)


def _merge_and_norm(x, gate_pre, ys, wbr_ref, wo_ref, ln1g_ref, ln1b_ref):
    acc = None
    for n in range(N_BRANCH):
        term = jax.nn.sigmoid(gate_pre(n)) * _dot(ys[n].astype(BF16), wbr_ref[0, n])
        acc = term if acc is None else acc + term
    mix = _dot(acc.astype(BF16), wo_ref[0])
    return _layer_norm(ALPHA * x + mix, ln1g_ref[0], ln1b_ref[0])


def _kv_kernel(mem_ref, w_ref, k_ref, v_ref):
    rows = mem_ref.shape[0]
    kv = _dot(mem_ref[...].astype(BF16), w_ref[0])
    for h in range(HEADS):
        k_ref[0, pl.ds(h, rows, stride=HEADS), :] = kv[:, h * HEAD_DIM:(h + 1) * HEAD_DIM]
        v_ref[0, pl.ds(h, rows, stride=HEADS), :] = kv[:, W + h * HEAD_DIM:W + (h + 1) * HEAD_DIM]


def _kv_project(mem2d, w_kv_bf16):
    rows = mem2d.shape[0]
    return pl.pallas_call(
        _kv_kernel,
        grid=(DEPTH,),
        in_specs=[
            pl.BlockSpec((rows, D_MODEL), lambda l: (0, 0)),
            pl.BlockSpec((1, D_MODEL, 2 * W), lambda l: (l, 0, 0)),
        ],
        out_specs=[
            pl.BlockSpec((1, rows * HEADS, HEAD_DIM), lambda l: (l, 0, 0)),
            pl.BlockSpec((1, rows * HEADS, HEAD_DIM), lambda l: (l, 0, 0)),
        ],
        out_shape=[jax.ShapeDtypeStruct((DEPTH, rows * HEADS, HEAD_DIM), F32)] * 2,
        compiler_params=pltpu.CompilerParams(
            dimension_semantics=("arbitrary",), vmem_limit_bytes=VMEM_LIMIT),
        name="kv_project",
    )(mem2d, w_kv_bf16)


def _conv_c_rows(ext_c, wb_ref, r0):
    lead = HALO_C - (CONV_C_W - 1)
    span = CONV_RB + HALO_C
    cols = []
    for c in range(W // LANES):
        lanes = slice(c * LANES, (c + 1) * LANES)
        win = ext_c[r0:r0 + span, lanes]
        acc = None
        for b in range(SUBLANES):
            sh = win if b == 0 else pltpu.roll(win, span - b, axis=0)
            for a in range(HALO_C // SUBLANES + 1):
                k = a * SUBLANES + b - lead
                if 0 <= k < CONV_C_W:
                    tap = sh[a * SUBLANES:a * SUBLANES + CONV_RB].reshape(CONV_RB // SUBLANES, SUBLANES, LANES)
                    term = wb_ref[k * SUBLANES:(k + 1) * SUBLANES, lanes][None] * tap
                    acc = term if acc is None else acc + term
        cols.append(acc.reshape(CONV_RB, LANES))
    return jnp.concatenate(cols, axis=1)


def _mixer_stage(z_ref, x_ref, mk_ref, mv_ref, wca_ref, lnvg_ref, lnvb_ref, ws_ref, bst_ref,
                 bcc_ref, lncg_ref, lncb_ref, wbr_ref, wo_ref, ln1g_ref, ln1b_ref,
                 y_ref, nba_ref, nbc_ref, ext_a, ext_c, wb_c):
    tm = z_ref.shape[0]
    n_chunks = tm // CHUNK

    glu = z_ref[:, 5 * W:6 * W] * jax.nn.sigmoid(z_ref[:, 6 * W:7 * W])
    ext_c[HALO_C:HALO_C + tm, :] = glu
    conv_c = jnp.concatenate(
        [_conv_c_rows(ext_c, wb_c, rb * CONV_RB) for rb in range(tm // CONV_RB)], axis=0) + bcc_ref[0]
    nbc_ref[0] = ext_c[HALO_C + tm - (CONV_C_W - 1):HALO_C + tm, :]
    ext_c[0:HALO_C, :] = ext_c[tm:tm + HALO_C, :]
    y_c = jax.nn.silu(_layer_norm(conv_c, lncg_ref[0], lncb_ref[0]))

    p = z_ref[:, 2 * W:3 * W] * z_ref[:, 0:W]
    ext_a[HALO_A:HALO_A + tm, :] = p
    wca = wca_ref[0]
    conv_a = (wca[0:1, :] * ext_a[HALO_A - 2:HALO_A - 2 + tm, :]
              + wca[1:2, :] * ext_a[HALO_A - 1:HALO_A - 1 + tm, :]
              + wca[2:3, :] * p)
    y_a = z_ref[:, W:2 * W] * conv_a
    nba_ref[0] = ext_a[HALO_A + tm - 2:HALO_A + tm, :]
    ext_a[0:HALO_A, :] = ext_a[tm:tm + HALO_A, :]

    u = _gelu_tanh(z_ref[:, 3 * W:4 * W])
    v = _layer_norm(_gelu_tanh(z_ref[:, 4 * W:5 * W]), lnvg_ref[0], lnvb_ref[0]).astype(BF16)
    row = lax.broadcasted_iota(jnp.int32, (CHUNK, CHUNK), 0)
    col = lax.broadcasted_iota(jnp.int32, (CHUNK, CHUNK), 1)
    causal = row >= col
    gate_cols = []
    for g in range(GROUPS):
        wg = jnp.where(causal, ws_ref[0, g], 0.0).astype(BF16)
        rhs = jnp.concatenate(
            [v[n * CHUNK:(n + 1) * CHUNK, g * GROUP_W:(g + 1) * GROUP_W] for n in range(n_chunks)],
            axis=1)
        gate_cols.append(_dot(wg, rhs) + bst_ref[0, :, g:g + 1])
    s = jnp.concatenate(
        [jnp.concatenate([gate_cols[g][:, n * GROUP_W:(n + 1) * GROUP_W] for g in range(GROUPS)], axis=1)
         for n in range(n_chunks)], axis=0)
    y_b = u * s

    heads = []
    for h in range(HEADS):
        qh = z_ref[:, 7 * W + h * HEAD_DIM:7 * W + (h + 1) * HEAD_DIM].astype(BF16)
        kh = mk_ref[0, 0, pl.ds(h, N_MEM, stride=HEADS), :].astype(BF16)
        vh = mv_ref[0, 0, pl.ds(h, N_MEM, stride=HEADS), :].astype(BF16)
        sc = _dot_nt(qh, kh) * ATTN_SCALE
        pr = _softmax_rows(sc).astype(BF16)
        heads.append(_dot(pr, vh))
    y_x = jnp.concatenate(heads, axis=1)

    gate_pre = lambda n: z_ref[:, 8 * W + n * D_MODEL:8 * W + (n + 1) * D_MODEL]
    y_ref[0] = _merge_and_norm(x_ref[0], gate_pre, (y_a, y_b, y_c, y_x), wbr_ref, wo_ref,
                               ln1g_ref, ln1b_ref)


def _prompt_mixer_kernel(x_ref, mk_ref, mv_ref, w_in_ref, wca_ref, lnvg_ref, lnvb_ref,
                         ws_ref, bst_ref, wcc_ref, bcc_ref, lncg_ref, lncb_ref, wbr_ref, wo_ref,
                         ln1g_ref, ln1b_ref,
                         y_ref, nba_ref, nbc_ref, z_ref, ext_a, ext_c, wb_c):
    @pl.when(pl.program_id(1) == 0)
    def _():
        ext_a[0:HALO_A, :] = jnp.zeros((HALO_A, W), F32)
        ext_c[0:HALO_C, :] = jnp.zeros((HALO_C, W), F32)
        for k in range(CONV_C_W):
            wb_c[k * SUBLANES:(k + 1) * SUBLANES, :] = jnp.broadcast_to(wcc_ref[0, k:k + 1, :], (SUBLANES, W))

    x = x_ref[0]
    xb = x.astype(BF16)
    for c0, c1 in ((5 * W, 7 * W), (0, 3 * W), (3 * W, 5 * W), (7 * W, 8 * W)):
        z_ref[:, c0:c1] = _dot(xb, w_in_ref[0, :, c0:c1])
    for c0 in range(8 * W, IN_COLS, D_MODEL):
        z_ref[:, c0:c0 + D_MODEL] = _dot(xb, w_in_ref[0, :, c0:c0 + D_MODEL])
    _mixer_stage(z_ref, x_ref, mk_ref, mv_ref, wca_ref, lnvg_ref, lnvb_ref, ws_ref,
                 bst_ref, bcc_ref, lncg_ref, lncb_ref, wbr_ref, wo_ref, ln1g_ref, ln1b_ref,
                 y_ref, nba_ref, nbc_ref, ext_a, ext_c, wb_c)


def _resident(shape_tail, layer):
    nd = len(shape_tail)
    return pl.BlockSpec((1,) + tuple(shape_tail), lambda *_: (layer,) + (0,) * nd,
                        pipeline_mode=pl.Buffered(1))


def _prompt_mixer(layer, x, mk, mv, p):
    bn, seq, _ = x.shape
    tm = TM_MIX
    return pl.pallas_call(
        _prompt_mixer_kernel,
        grid=(bn, seq // tm),
        in_specs=[
            pl.BlockSpec((1, tm, D_MODEL), lambda b, j: (b, j, 0)),
            pl.BlockSpec((1, 1, N_MEM * HEADS, HEAD_DIM), lambda b, j: (layer, b, 0, 0)),
            pl.BlockSpec((1, 1, N_MEM * HEADS, HEAD_DIM), lambda b, j: (layer, b, 0, 0)),
            _resident((D_MODEL, IN_COLS), layer),
            _resident((CONV_A_W, W), layer),
            _resident((1, W), layer),
            _resident((1, W), layer),
            _resident((GROUPS, CHUNK, CHUNK), layer),
            _resident((CHUNK, GROUPS), layer),
            _resident((CONV_C_W, W), layer),
            _resident((1, W), layer),
            _resident((1, W), layer),
            _resident((1, W), layer),
            _resident((N_BRANCH, W, D_MODEL), layer),
            _resident((D_MODEL, D_MODEL), layer),
            _resident((1, D_MODEL), layer),
            _resident((1, D_MODEL), layer),
        ],
        out_specs=[
            pl.BlockSpec((1, tm, D_MODEL), lambda b, j: (b, j, 0)),
            pl.BlockSpec((1, CONV_A_W - 1, W), lambda b, j: (b, 0, 0)),
            pl.BlockSpec((1, CONV_C_W - 1, W), lambda b, j: (b, 0, 0)),
        ],
        out_shape=[
            jax.ShapeDtypeStruct((bn, seq, D_MODEL), F32),
            jax.ShapeDtypeStruct((bn, CONV_A_W - 1, W), F32),
            jax.ShapeDtypeStruct((bn, CONV_C_W - 1, W), F32),
        ],
        scratch_shapes=[
            pltpu.VMEM((tm, IN_COLS), F32),
            pltpu.VMEM((tm + HALO_A, W), F32),
            pltpu.VMEM((tm + HALO_C, W), F32),
            pltpu.VMEM((CONV_C_W * SUBLANES, W), F32),
        ],
        compiler_params=pltpu.CompilerParams(
            dimension_semantics=("arbitrary", "arbitrary"), vmem_limit_bytes=VMEM_LIMIT),
        name="prompt_mixer",
    )(x, mk, mv, p["w_in"], p["w_conv_a"], p["ln_v_g"], p["ln_v_b"], p["w_s"], p["b_s_t"],
      p["w_conv_c"], p["b_conv_c"], p["ln_c_g"], p["ln_c_b"], p["w_out_br"], p["w_o"],
      p["ln1_g"], p["ln1_b"])


def _sample_attn_kernel(x_ref, wq_ref, k_ref, v_ref, o_ref):
    n_pos, sb, _ = x_ref.shape
    rows = n_pos * sb
    xb = x_ref[...].reshape(rows, D_MODEL).astype(BF16)
    q = _dot(xb, wq_ref[0])
    r_seq = lax.broadcasted_iota(jnp.int32, (rows, sb * N_MEM), 0) % sb
    c_seq = lax.broadcasted_iota(jnp.int32, (rows, sb * N_MEM), 1) // N_MEM
    own = r_seq == c_seq
    heads = []
    for h in range(HEADS):
        sl = slice(h * HEAD_DIM, (h + 1) * HEAD_DIM)
        kh = k_ref[:, pl.ds(h, N_MEM, stride=HEADS), :].reshape(sb * N_MEM, HEAD_DIM).astype(BF16)
        vh = v_ref[:, pl.ds(h, N_MEM, stride=HEADS), :].reshape(sb * N_MEM, HEAD_DIM).astype(BF16)
        sc = _dot_nt(q[:, sl].astype(BF16), kh) * ATTN_SCALE
        sc = jnp.where(own, sc, -jnp.inf)
        pr = _softmax_rows(sc).astype(BF16)
        heads.append(_dot(pr, vh))
    o_ref[...] = jnp.concatenate(heads, axis=1).reshape(n_pos, sb, W)


def _sample_attn(layer, x_t, wq, cache_k, cache_v):
    n_pos, n_seq, _ = x_t.shape
    sb = SEQ_BLK
    return pl.pallas_call(
        _sample_attn_kernel,
        grid=(n_seq // sb,),
        in_specs=[
            pl.BlockSpec((n_pos, sb, D_MODEL), lambda i: (0, i, 0)),
            pl.BlockSpec((1, D_MODEL, W), lambda i: (layer, 0, 0)),
            pl.BlockSpec((None, sb, N_MEM * HEADS, HEAD_DIM), lambda i: (layer, i, 0, 0)),
            pl.BlockSpec((None, sb, N_MEM * HEADS, HEAD_DIM), lambda i: (layer, i, 0, 0)),
        ],
        out_specs=pl.BlockSpec((n_pos, sb, W), lambda i: (0, i, 0)),
        out_shape=jax.ShapeDtypeStruct((n_pos, n_seq, W), F32),
        compiler_params=pltpu.CompilerParams(
            dimension_semantics=("arbitrary",), vmem_limit_bytes=VMEM_LIMIT),
        name="sample_attn",
    )(x_t, wq, cache_k, cache_v)


def _sample_mixer_kernel(x_ref, yx_ref, sa_ref, sc_ref, w_in_ref, wca_ref, lnvg_ref, lnvb_ref,
                         wsr_ref, bsr_ref, wcc_ref, bcc_ref, lncg_ref, lncb_ref, wbr_ref, wo_ref,
                         ln1g_ref, ln1b_ref,
                         y_ref, p_ref, nsc_ref, v_ref):
    n_pos, n_seq, _ = x_ref.shape
    rows = n_pos * n_seq
    x = x_ref[...].reshape(rows, D_MODEL)
    xb = x.astype(BF16)

    def proj(c0, c1):
        return _dot(xb, w_in_ref[0, :, c0:c1])

    def pos(a, t):
        return a[t * n_seq:(t + 1) * n_seq, :]

    za = proj(0, 3 * W)
    p = za[:, 2 * W:3 * W] * za[:, 0:W]
    p_ref[...] = p.reshape(n_pos, n_seq, W)
    wca = wca_ref[0]
    hist_a = [sa_ref[0, r] for r in range(CONV_A_W - 1)] + [pos(p, t) for t in range(n_pos)]
    conv_a = jnp.concatenate(
        [sum(wca[k:k + 1, :] * hist_a[t + k] for k in range(CONV_A_W)) for t in range(n_pos)], axis=0)
    y_a = za[:, W:2 * W] * conv_a

    zb = _gelu_tanh(proj(3 * W, 5 * W))
    u = zb[:, :W]
    v = _layer_norm(zb[:, W:], lnvg_ref[0], lnvb_ref[0])
    v_ref[...] = v.reshape(n_pos, n_seq, W)
    gate_rows = []
    for t in range(n_pos):
        acc = bsr_ref[0, t:t + 1, :]
        for s_ in range(t + 1):
            acc = acc + wsr_ref[0, t * n_pos + s_:t * n_pos + s_ + 1, :] * pos(v, s_)
        gate_rows.append(acc)
    y_b = u * jnp.concatenate(gate_rows, axis=0)

    zc = proj(5 * W, 7 * W)
    glu = zc[:, :W] * jax.nn.sigmoid(zc[:, W:])
    keep = CONV_C_W - 1 - n_pos
    nsc_ref[:, 0:keep, :] = sc_ref[0, :, n_pos:CONV_C_W - 1, :]
    for t in range(n_pos):
        nsc_ref[:, keep + t, :] = pos(glu, t)
    wcc = wcc_ref[0]
    conv_rows = []
    for t in range(n_pos):
        acc = None
        for k in range(CONV_C_W):
            r = t + k
            src = sc_ref[0, :, r, :] if r < CONV_C_W - 1 else pos(glu, r - (CONV_C_W - 1))
            term = wcc[k:k + 1, :] * src
            acc = term if acc is None else acc + term
        conv_rows.append(acc)
    conv_c = jnp.concatenate(conv_rows, axis=0) + bcc_ref[0]
    y_c = jax.nn.silu(_layer_norm(conv_c, lncg_ref[0], lncb_ref[0]))

    y_x = yx_ref[...].reshape(rows, W)
    gate_pre = lambda n: proj(8 * W + n * D_MODEL, 8 * W + (n + 1) * D_MODEL)
    y = _merge_and_norm(x, gate_pre, (y_a, y_b, y_c, y_x), wbr_ref, wo_ref, ln1g_ref, ln1b_ref)
    y_ref[...] = y.reshape(n_pos, n_seq, D_MODEL)


def _sample_mixer(layer, x_t, yx_t, state_a_t, state_c, p):
    n_pos, n_seq, _ = x_t.shape
    sb = SEQ_BLK_MIX

    def by_pos(width):
        return pl.BlockSpec((n_pos, sb, width), lambda i: (0, i, 0))

    layer_slice = functools.partial(_resident, layer=layer)

    return pl.pallas_call(
        _sample_mixer_kernel,
        grid=(n_seq // sb,),
        in_specs=[
            by_pos(D_MODEL),
            by_pos(W),
            pl.BlockSpec((1, CONV_A_W - 1, sb, W), lambda i: (layer, 0, i, 0)),
            pl.BlockSpec((1, sb, CONV_C_W - 1, W), lambda i: (layer, i, 0, 0)),
            layer_slice((D_MODEL, IN_COLS)),
            layer_slice((CONV_A_W, W)),
            layer_slice((1, W)),
            layer_slice((1, W)),
            layer_slice((n_pos * n_pos, W)),
            layer_slice((n_pos, W)),
            layer_slice((CONV_C_W, W)),
            layer_slice((1, W)),
            layer_slice((1, W)),
            layer_slice((1, W)),
            layer_slice((N_BRANCH, W, D_MODEL)),
            layer_slice((D_MODEL, D_MODEL)),
            layer_slice((1, D_MODEL)),
            layer_slice((1, D_MODEL)),
        ],
        out_specs=[
            by_pos(D_MODEL),
            by_pos(W),
            pl.BlockSpec((sb, CONV_C_W - 1, W), lambda i: (i, 0, 0)),
            by_pos(W),
        ],
        out_shape=[
            jax.ShapeDtypeStruct((n_pos, n_seq, D_MODEL), F32),
            jax.ShapeDtypeStruct((n_pos, n_seq, W), F32),
            jax.ShapeDtypeStruct((n_seq, CONV_C_W - 1, W), F32),
            jax.ShapeDtypeStruct((n_pos, n_seq, W), F32),
        ],
        compiler_params=pltpu.CompilerParams(
            dimension_semantics=("arbitrary",), vmem_limit_bytes=VMEM_LIMIT),
        name="sample_mixer",
    )(x_t, yx_t, state_a_t, state_c, p["w_in"], p["w_conv_a"], p["ln_v_g"], p["ln_v_b"],
      p["w_s_rows"], p["b_s_rows"], p["w_conv_c"], p["b_conv_c"], p["ln_c_g"], p["ln_c_b"],
      p["w_out_br"], p["w_o"], p["ln1_g"], p["ln1_b"])


def _mlp_kernel(x_ref, wu_ref, bu_ref, wd_ref, bd_ref, g_ref, b_ref, y_ref):
    x = x_ref[...]
    h = jnp.maximum(_dot(x.astype(BF16), wu_ref[0]) + bu_ref[0], 0.0)
    h = _dot((h * h).astype(BF16), wd_ref[0]) + bd_ref[0]
    y_ref[...] = _layer_norm(ALPHA * x + h, g_ref[0], b_ref[0])


def _mlp(layer, x2d, p):
    rows = x2d.shape[0]
    tm = min(TM_MLP, rows)
    return pl.pallas_call(
        _mlp_kernel,
        grid=(rows // tm,),
        in_specs=[
            pl.BlockSpec((tm, D_MODEL), lambda i: (i, 0)),
            _resident((D_MODEL, D_FF), layer),
            _resident((1, D_FF), layer),
            _resident((D_FF, D_MODEL), layer),
            _resident((1, D_MODEL), layer),
            _resident((1, D_MODEL), layer),
            _resident((1, D_MODEL), layer),
        ],
        out_specs=pl.BlockSpec((tm, D_MODEL), lambda i: (i, 0)),
        out_shape=jax.ShapeDtypeStruct((rows, D_MODEL), F32),
        compiler_params=pltpu.CompilerParams(
            dimension_semantics=("arbitrary",), vmem_limit_bytes=VMEM_LIMIT),
        name="mlp",
    )(x2d, p["w_up"], p["b_up"], p["w_down"], p["b_down"], p["ln2_g"], p["ln2_b"])


def kernel(x_prompt, x_sample, mem_prompt, state_conv_a, state_conv_c, cache_mem_k, cache_mem_v,
           w_in, w_conv_a, ln_v_g, ln_v_b, w_s, b_s, w_conv_c, b_conv_c, ln_c_g, ln_c_b,
           w_mem_kv, w_out_br, w_o, ln1_g, ln1_b, w_up, b_up, w_down, b_down, ln2_g, ln2_b):
    bp, seq, _ = x_prompt.shape
    n_seq, n_pos, _ = x_sample.shape
    assert seq % TM_MIX == 0 and TM_MIX % CHUNK == 0 and TM_MIX % CONV_RB == 0
    assert n_seq % SEQ_BLK == 0 and n_seq % SEQ_BLK_MIX == 0 and n_pos <= CHUNK and n_pos >= CONV_A_W - 1

    row = lambda a: a[:, None, :]
    w_in_bf = w_in.astype(BF16)
    params = {
        "w_in": w_in_bf,
        "w_conv_a": w_conv_a, "ln_v_g": row(ln_v_g), "ln_v_b": row(ln_v_b),
        "w_s": w_s, "b_s_t": jnp.swapaxes(b_s, 1, 2),
        "w_s_rows": jnp.repeat(
            jnp.transpose(w_s[:, :, :n_pos, :n_pos], (0, 2, 3, 1)).reshape(DEPTH, n_pos * n_pos, GROUPS),
            GROUP_W, axis=-1),
        "b_s_rows": jnp.repeat(jnp.swapaxes(b_s[:, :, :n_pos], 1, 2), GROUP_W, axis=-1),
        "w_conv_c": w_conv_c, "b_conv_c": row(b_conv_c), "ln_c_g": row(ln_c_g), "ln_c_b": row(ln_c_b),
        "w_out_br": w_out_br.astype(BF16), "w_o": w_o.astype(BF16),
        "ln1_g": row(ln1_g), "ln1_b": row(ln1_b),
        "w_up": w_up.astype(BF16), "b_up": row(b_up), "w_down": w_down.astype(BF16),
        "b_down": row(b_down), "ln2_g": row(ln2_g), "ln2_b": row(ln2_b),
    }
    w_q = w_in_bf[:, :, 7 * W:8 * W]

    pk, pv = _kv_project(mem_prompt.reshape(bp * N_MEM, D_MODEL), w_mem_kv.astype(BF16))
    pk = pk.reshape(DEPTH, bp, N_MEM * HEADS, HEAD_DIM)
    pv = pv.reshape(DEPTH, bp, N_MEM * HEADS, HEAD_DIM)

    ys = jnp.swapaxes(x_sample, 0, 1)
    state_a_t = jnp.swapaxes(state_conv_a, 1, 2)
    cache_k = cache_mem_k.reshape(DEPTH, n_seq, N_MEM * HEADS, HEAD_DIM)
    cache_v = cache_mem_v.reshape(DEPTH, n_seq, N_MEM * HEADS, HEAD_DIM)

    yp = x_prompt
    pa, pc, sa, sc, sv = [], [], [], [], []
    for l in range(DEPTH):
        x1, nba, nbc = _prompt_mixer(l, yp, pk, pv, params)
        yp = _mlp(l, x1.reshape(bp * seq, D_MODEL), params).reshape(bp, seq, D_MODEL)
        pa.append(nba)
        pc.append(nbc)

        yx = _sample_attn(l, ys, w_q, cache_k, cache_v)
        s1, p_t, nsc, v_t = _sample_mixer(l, ys, yx, state_a_t, state_conv_c, params)
        ys = _mlp(l, s1.reshape(n_pos * n_seq, D_MODEL), params).reshape(n_pos, n_seq, D_MODEL)
        sa.append(jnp.swapaxes(p_t[n_pos - (CONV_A_W - 1):], 0, 1))
        sc.append(nsc)
        sv.append(jnp.swapaxes(v_t, 0, 1))

    return (yp, jnp.swapaxes(ys, 0, 1), jnp.stack(pa), jnp.stack(pc),
            pk.reshape(DEPTH, bp, N_MEM, HEADS, HEAD_DIM), pv.reshape(DEPTH, bp, N_MEM, HEADS, HEAD_DIM),
            jnp.stack(sa), jnp.stack(sc), jnp.stack(sv))
```

```python
import functools
import math

import jax
import jax.numpy as jnp
from jax import lax
from jax.experimental import pallas as pl
from jax.experimental.pallas import tpu as pltpu

D_MODEL = 1024
DEPTH = 4
N_MEM = 256
W = D_MODEL // 2
CONV_A_W = 3
CHUNK = 128
GROUPS = 4
GROUP_W = W // GROUPS
CONV_C_W = 31
HEADS = 4
HEAD_DIM = W // HEADS
D_FF = 4 * D_MODEL
N_BRANCH = 4
IN_COLS = 8 * W + N_BRANCH * D_MODEL
ALPHA = (2 * DEPTH) ** 0.25
LN_EPS = 1e-5
ATTN_SCALE = HEAD_DIM ** -0.5

V7X_VMEM_BYTES = 64 * 1024 * 1024
VMEM_LIMIT = V7X_VMEM_BYTES - 8 * 1024 * 1024
SUBLANES = 8
BF16_SUBLANES = 16
LANES = 128

TM_MIX = 256
TM_MLP = 512
HALO_A = 8
HALO_C = 32
CONV_RB = 64
SEQ_BLK = 8
SEQ_BLK_MIX = 32

F32 = jnp.float32
BF16 = jnp.bfloat16


def _layer_norm(x, g, b):
    mu = jnp.mean(x, axis=-1, keepdims=True)
    xc = x - mu
    var = jnp.mean(xc * xc, axis=-1, keepdims=True)
    return xc * lax.rsqrt(var + LN_EPS) * g + b


def _dot(a, b):
    return jnp.dot(a, b, preferred_element_type=F32)


def _dot_nt(a, b):
    return lax.dot_general(a, b, (((1,), (1,)), ((), ())), preferred_element_type=F32)


def _gelu_tanh(x):
    c1 = math.sqrt(2.0 / math.pi)
    u = x * (c1 + (c1 * 0.044715) * (x * x))
    h = 0.5 * x
    return h + h * jnp.tanh(u)


def _softmax_rows(s):
    m = jnp.max(s, axis=-1, keepdims=True)
    e = jnp.exp(s - m)
    return e / jnp.sum(e, axis=-1, keepdims=True)


def _merge_and_norm(x, gate_pre, ys, wbr_ref, wo_ref, ln1g_ref, ln1b_ref):
    acc = None
    for n in range(N_BRANCH):
        term = jax.nn.sigmoid(gate_pre(n)) * _dot(ys[n].astype(BF16), wbr_ref[0, n])
        acc = term if acc is None else acc + term
    mix = _dot(acc.astype(BF16), wo_ref[0])
    return _layer_norm(ALPHA * x + mix, ln1g_ref[0], ln1b_ref[0])


def _kv_kernel(mem_ref, w_ref, k_ref, v_ref):
    rows = mem_ref.shape[0]
    kv = _dot(mem_ref[...].astype(BF16), w_ref[0].astype(BF16))
    for h in range(HEADS):
        k_ref[0, pl.ds(h, rows, stride=HEADS), :] = kv[:, h * HEAD_DIM:(h + 1) * HEAD_DIM]
        v_ref[0, pl.ds(h, rows, stride=HEADS), :] = kv[:, W + h * HEAD_DIM:W + (h + 1) * HEAD_DIM]


def _kv_project(mem2d, w_kv):
    rows = mem2d.shape[0]
    return pl.pallas_call(
        _kv_kernel,
        grid=(DEPTH,),
        in_specs=[
            pl.BlockSpec((rows, D_MODEL), lambda l: (0, 0)),
            pl.BlockSpec((1, D_MODEL, 2 * W), lambda l: (l, 0, 0)),
        ],
        out_specs=[
            pl.BlockSpec((1, rows * HEADS, HEAD_DIM), lambda l: (l, 0, 0)),
            pl.BlockSpec((1, rows * HEADS, HEAD_DIM), lambda l: (l, 0, 0)),
        ],
        out_shape=[jax.ShapeDtypeStruct((DEPTH, rows * HEADS, HEAD_DIM), F32)] * 2,
        compiler_params=pltpu.CompilerParams(
            dimension_semantics=("arbitrary",), vmem_limit_bytes=VMEM_LIMIT),
        name="kv_project",
    )(mem2d, w_kv)


def _conv_c_rows(ext_c, wb_ref, r0):
    lead = HALO_C - (CONV_C_W - 1)
    span = CONV_RB + HALO_C
    cols = []
    for c in range(W // LANES):
        lanes = slice(c * LANES, (c + 1) * LANES)
        win = ext_c[r0:r0 + span, lanes]
        acc = None
        for b in range(SUBLANES):
            sh = win if b == 0 else pltpu.roll(win, span - b, axis=0)
            for a in range(HALO_C // SUBLANES + 1):
                k = a * SUBLANES + b - lead
                if 0 <= k < CONV_C_W:
                    tap = sh[a * SUBLANES:a * SUBLANES + CONV_RB].reshape(CONV_RB // SUBLANES, SUBLANES, LANES)
                    term = wb_ref[k * SUBLANES:(k + 1) * SUBLANES, lanes][None] * tap
                    acc = term if acc is None else acc + term
        cols.append(acc.reshape(CONV_RB, LANES))
    return jnp.concatenate(cols, axis=1)


def _mixer_stage(z_ref, x_ref, mk_ref, mv_ref, wca_ref, lnvg_ref, lnvb_ref, ws_ref, bst_ref,
                 bcc_ref, lncg_ref, lncb_ref, wbr_ref, wo_ref, ln1g_ref, ln1b_ref,
                 y_ref, nba_ref, nbc_ref, ext_a, ext_c, wb_c):
    tm = z_ref.shape[0]
    n_chunks = tm // CHUNK

    glu = z_ref[:, 5 * W:6 * W] * jax.nn.sigmoid(z_ref[:, 6 * W:7 * W])
    ext_c[HALO_C:HALO_C + tm, :] = glu
    conv_c = jnp.concatenate(
        [_conv_c_rows(ext_c, wb_c, rb * CONV_RB) for rb in range(tm // CONV_RB)], axis=0) + bcc_ref[0]
    nbc_ref[0] = ext_c[HALO_C + tm - (CONV_C_W - 1):HALO_C + tm, :]
    ext_c[0:HALO_C, :] = ext_c[tm:tm + HALO_C, :]
    y_c = jax.nn.silu(_layer_norm(conv_c, lncg_ref[0], lncb_ref[0]))

    p = z_ref[:, 2 * W:3 * W] * z_ref[:, 0:W]
    ext_a[HALO_A:HALO_A + tm, :] = p
    wca = wca_ref[0]
    conv_a = (wca[0:1, :] * ext_a[HALO_A - 2:HALO_A - 2 + tm, :]
              + wca[1:2, :] * ext_a[HALO_A - 1:HALO_A - 1 + tm, :]
              + wca[2:3, :] * p)
    y_a = z_ref[:, W:2 * W] * conv_a
    nba_ref[0] = ext_a[HALO_A + tm - 2:HALO_A + tm, :]
    ext_a[0:HALO_A, :] = ext_a[tm:tm + HALO_A, :]

    u = _gelu_tanh(z_ref[:, 3 * W:4 * W])
    v = _layer_norm(_gelu_tanh(z_ref[:, 4 * W:5 * W]), lnvg_ref[0], lnvb_ref[0]).astype(BF16)
    row = lax.broadcasted_iota(jnp.int32, (CHUNK, CHUNK), 0)
    col = lax.broadcasted_iota(jnp.int32, (CHUNK, CHUNK), 1)
    causal = row >= col
    gate_cols = []
    for g in range(GROUPS):
        wg = jnp.where(causal, ws_ref[0, g], 0.0).astype(BF16)
        rhs = jnp.concatenate(
            [v[n * CHUNK:(n + 1) * CHUNK, g * GROUP_W:(g + 1) * GROUP_W] for n in range(n_chunks)],
            axis=1)
        gate_cols.append(_dot(wg, rhs) + bst_ref[0, :, g:g + 1])
    s = jnp.concatenate(
        [jnp.concatenate([gate_cols[g][:, n * GROUP_W:(n + 1) * GROUP_W] for g in range(GROUPS)], axis=1)
         for n in range(n_chunks)], axis=0)
    y_b = u * s

    heads = []
    for h in range(HEADS):
        qh = z_ref[:, 7 * W + h * HEAD_DIM:7 * W + (h + 1) * HEAD_DIM].astype(BF16)
        kh = mk_ref[0, 0, pl.ds(h, N_MEM, stride=HEADS), :].astype(BF16)
        vh = mv_ref[0, 0, pl.ds(h, N_MEM, stride=HEADS), :].astype(BF16)
        sc = _dot_nt(qh, kh) * ATTN_SCALE
        pr = _softmax_rows(sc).astype(BF16)
        heads.append(_dot(pr, vh))
    y_x = jnp.concatenate(heads, axis=1)

    gate_pre = lambda n: z_ref[:, 8 * W + n * D_MODEL:8 * W + (n + 1) * D_MODEL]
    y_ref[0] = _merge_and_norm(x_ref[0], gate_pre, (y_a, y_b, y_c, y_x), wbr_ref, wo_ref,
                               ln1g_ref, ln1b_ref)


def _prompt_mixer_kernel(x_ref, mk_ref, mv_ref, w_in_ref, wca_ref, lnvg_ref, lnvb_ref,
                         ws_ref, bst_ref, wcc_ref, bcc_ref, lncg_ref, lncb_ref, wbr_ref, wo_ref,
                         ln1g_ref, ln1b_ref, wu32_ref, wd32_ref,
                         y_ref, nba_ref, nbc_ref, wu16_ref, wd16_ref, z_ref, ext_a, ext_c, wb_c):
    wu16_ref[...] = wu32_ref[...].astype(BF16)
    wd16_ref[...] = wd32_ref[...].astype(BF16)

    @pl.when(pl.program_id(1) == 0)
    def _():
        ext_a[0:HALO_A, :] = jnp.zeros((HALO_A, W), F32)
        ext_c[0:HALO_C, :] = jnp.zeros((HALO_C, W), F32)
        for k in range(CONV_C_W):
            wb_c[k * SUBLANES:(k + 1) * SUBLANES, :] = jnp.broadcast_to(wcc_ref[0, k:k + 1, :], (SUBLANES, W))

    x = x_ref[0]
    xb = x.astype(BF16)
    for c0, c1 in ((5 * W, 7 * W), (0, 3 * W), (3 * W, 5 * W), (7 * W, 8 * W)):
        z_ref[:, c0:c1] = _dot(xb, w_in_ref[0, :, c0:c1])
    for c0 in range(8 * W, IN_COLS, D_MODEL):
        z_ref[:, c0:c0 + D_MODEL] = _dot(xb, w_in_ref[0, :, c0:c0 + D_MODEL])
    _mixer_stage(z_ref, x_ref, mk_ref, mv_ref, wca_ref, lnvg_ref, lnvb_ref, ws_ref,
                 bst_ref, bcc_ref, lncg_ref, lncb_ref, wbr_ref, wo_ref, ln1g_ref, ln1b_ref,
                 y_ref, nba_ref, nbc_ref, ext_a, ext_c, wb_c)


def _resident(shape_tail, layer):
    nd = len(shape_tail)
    return pl.BlockSpec((1,) + tuple(shape_tail), lambda *_: (layer,) + (0,) * nd,
                        pipeline_mode=pl.Buffered(1))


def _prompt_mixer(layer, x, mk, mv, p, big, w_up, w_down):
    bn, seq, _ = x.shape
    tm = TM_MIX
    tiles = seq // tm
    n_steps = bn * tiles
    up_rows, down_rows = D_MODEL // n_steps, D_FF // n_steps
    assert up_rows * n_steps == D_MODEL and up_rows % BF16_SUBLANES == 0
    return pl.pallas_call(
        _prompt_mixer_kernel,
        grid=(bn, tiles),
        in_specs=[
            pl.BlockSpec((1, tm, D_MODEL), lambda b, j: (b, j, 0)),
            pl.BlockSpec((1, 1, N_MEM * HEADS, HEAD_DIM), lambda b, j: (layer, b, 0, 0)),
            pl.BlockSpec((1, 1, N_MEM * HEADS, HEAD_DIM), lambda b, j: (layer, b, 0, 0)),
            _resident((D_MODEL, IN_COLS), 0),
            _resident((CONV_A_W, W), layer),
            _resident((1, W), layer),
            _resident((1, W), layer),
            _resident((GROUPS, CHUNK, CHUNK), layer),
            _resident((CHUNK, GROUPS), layer),
            _resident((CONV_C_W, W), layer),
            _resident((1, W), layer),
            _resident((1, W), layer),
            _resident((1, W), layer),
            _resident((N_BRANCH, W, D_MODEL), 0),
            _resident((D_MODEL, D_MODEL), 0),
            _resident((1, D_MODEL), layer),
            _resident((1, D_MODEL), layer),
            pl.BlockSpec((1, up_rows, D_FF), lambda b, j: (layer, b * tiles + j, 0)),
            pl.BlockSpec((1, down_rows, D_MODEL), lambda b, j: (layer, b * tiles + j, 0)),
        ],
        out_specs=[
            pl.BlockSpec((1, tm, D_MODEL), lambda b, j: (b, j, 0)),
            pl.BlockSpec((1, CONV_A_W - 1, W), lambda b, j: (b, 0, 0)),
            pl.BlockSpec((1, CONV_C_W - 1, W), lambda b, j: (b, 0, 0)),
            pl.BlockSpec((1, up_rows, D_FF), lambda b, j: (0, b * tiles + j, 0)),
            pl.BlockSpec((1, down_rows, D_MODEL), lambda b, j: (0, b * tiles + j, 0)),
        ],
        out_shape=[
            jax.ShapeDtypeStruct((bn, seq, D_MODEL), F32),
            jax.ShapeDtypeStruct((bn, CONV_A_W - 1, W), F32),
            jax.ShapeDtypeStruct((bn, CONV_C_W - 1, W), F32),
            jax.ShapeDtypeStruct((1, D_MODEL, D_FF), BF16),
            jax.ShapeDtypeStruct((1, D_FF, D_MODEL), BF16),
        ],
        scratch_shapes=[
            pltpu.VMEM((tm, IN_COLS), F32),
            pltpu.VMEM((tm + HALO_A, W), F32),
            pltpu.VMEM((tm + HALO_C, W), F32),
            pltpu.VMEM((CONV_C_W * SUBLANES, W), F32),
        ],
        compiler_params=pltpu.CompilerParams(
            dimension_semantics=("arbitrary", "arbitrary"), vmem_limit_bytes=VMEM_LIMIT),
        name="prompt_mixer",
    )(x, mk, mv, big["w_in"], p["w_conv_a"], p["ln_v_g"], p["ln_v_b"], p["w_s"], p["b_s_t"],
      p["w_conv_c"], p["b_conv_c"], p["ln_c_g"], p["ln_c_b"], big["w_out_br"], big["w_o"],
      p["ln1_g"], p["ln1_b"], w_up, w_down)


def _sample_attn_kernel(x_ref, wq_ref, k_ref, v_ref, o_ref):
    n_pos, sb, _ = x_ref.shape
    rows = n_pos * sb
    xb = x_ref[...].reshape(rows, D_MODEL).astype(BF16)
    q = _dot(xb, wq_ref[0])
    r_seq = lax.broadcasted_iota(jnp.int32, (rows, sb * N_MEM), 0) % sb
    c_seq = lax.broadcasted_iota(jnp.int32, (rows, sb * N_MEM), 1) // N_MEM
    own = r_seq == c_seq
    heads = []
    for h in range(HEADS):
        sl = slice(h * HEAD_DIM, (h + 1) * HEAD_DIM)
        kh = k_ref[:, pl.ds(h, N_MEM, stride=HEADS), :].reshape(sb * N_MEM, HEAD_DIM).astype(BF16)
        vh = v_ref[:, pl.ds(h, N_MEM, stride=HEADS), :].reshape(sb * N_MEM, HEAD_DIM).astype(BF16)
        sc = _dot_nt(q[:, sl].astype(BF16), kh) * ATTN_SCALE
        sc = jnp.where(own, sc, -jnp.inf)
        pr = _softmax_rows(sc).astype(BF16)
        heads.append(_dot(pr, vh))
    o_ref[...] = jnp.concatenate(heads, axis=1).reshape(n_pos, sb, W)


def _sample_attn(layer, x_t, w_in_bf, cache_k, cache_v):
    n_pos, n_seq, _ = x_t.shape
    sb = SEQ_BLK
    q_block = (7 * W) // W
    return pl.pallas_call(
        _sample_attn_kernel,
        grid=(n_seq // sb,),
        in_specs=[
            pl.BlockSpec((n_pos, sb, D_MODEL), lambda i: (0, i, 0)),
            pl.BlockSpec((1, D_MODEL, W), lambda i: (0, 0, q_block)),
            pl.BlockSpec((None, sb, N_MEM * HEADS, HEAD_DIM), lambda i: (layer, i, 0, 0)),
            pl.BlockSpec((None, sb, N_MEM * HEADS, HEAD_DIM), lambda i: (layer, i, 0, 0)),
        ],
        out_specs=pl.BlockSpec((n_pos, sb, W), lambda i: (0, i, 0)),
        out_shape=jax.ShapeDtypeStruct((n_pos, n_seq, W), F32),
        compiler_params=pltpu.CompilerParams(
            dimension_semantics=("arbitrary",), vmem_limit_bytes=VMEM_LIMIT),
        name="sample_attn",
    )(x_t, w_in_bf, cache_k, cache_v)


def _sample_mixer_kernel(x_ref, yx_ref, sa_ref, sc_ref, w_in_ref, wca_ref, lnvg_ref, lnvb_ref,
                         wsr_ref, bsr_ref, wcc_ref, bcc_ref, lncg_ref, lncb_ref, wbr_ref, wo_ref,
                         ln1g_ref, ln1b_ref,
                         y_ref, p_ref, nsc_ref, v_ref):
    n_pos, n_seq, _ = x_ref.shape
    rows = n_pos * n_seq
    x = x_ref[...].reshape(rows, D_MODEL)
    xb = x.astype(BF16)

    def proj(c0, c1):
        return _dot(xb, w_in_ref[0, :, c0:c1])

    def pos(a, t):
        return a[t * n_seq:(t + 1) * n_seq, :]

    za = proj(0, 3 * W)
    p = za[:, 2 * W:3 * W] * za[:, 0:W]
    p_ref[...] = p.reshape(n_pos, n_seq, W)
    wca = wca_ref[0]
    hist_a = [sa_ref[0, r] for r in range(CONV_A_W - 1)] + [pos(p, t) for t in range(n_pos)]
    conv_a = jnp.concatenate(
        [sum(wca[k:k + 1, :] * hist_a[t + k] for k in range(CONV_A_W)) for t in range(n_pos)], axis=0)
    y_a = za[:, W:2 * W] * conv_a

    zb = _gelu_tanh(proj(3 * W, 5 * W))
    u = zb[:, :W]
    v = _layer_norm(zb[:, W:], lnvg_ref[0], lnvb_ref[0])
    v_ref[...] = v.reshape(n_pos, n_seq, W)
    gate_rows = []
    for t in range(n_pos):
        acc = bsr_ref[0, t:t + 1, :]
        for s_ in range(t + 1):
            acc = acc + wsr_ref[0, t * n_pos + s_:t * n_pos + s_ + 1, :] * pos(v, s_)
        gate_rows.append(acc)
    y_b = u * jnp.concatenate(gate_rows, axis=0)

    zc = proj(5 * W, 7 * W)
    glu = zc[:, :W] * jax.nn.sigmoid(zc[:, W:])
    keep = CONV_C_W - 1 - n_pos
    nsc_ref[0:keep] = sc_ref[0, n_pos:CONV_C_W - 1]
    nsc_ref[keep:CONV_C_W - 1] = glu.reshape(n_pos, n_seq, W)
    wcc = wcc_ref[0]
    conv_rows = []
    for t in range(n_pos):
        acc = None
        for k in range(CONV_C_W):
            r = t + k
            src = sc_ref[0, r] if r < CONV_C_W - 1 else pos(glu, r - (CONV_C_W - 1))
            term = wcc[k:k + 1, :] * src
            acc = term if acc is None else acc + term
        conv_rows.append(acc)
    conv_c = jnp.concatenate(conv_rows, axis=0) + bcc_ref[0]
    y_c = jax.nn.silu(_layer_norm(conv_c, lncg_ref[0], lncb_ref[0]))

    y_x = yx_ref[...].reshape(rows, W)
    gate_pre = lambda n: proj(8 * W + n * D_MODEL, 8 * W + (n + 1) * D_MODEL)
    y = _merge_and_norm(x, gate_pre, (y_a, y_b, y_c, y_x), wbr_ref, wo_ref, ln1g_ref, ln1b_ref)
    y_ref[...] = y.reshape(n_pos, n_seq, D_MODEL)


def _sample_mixer(layer, x_t, yx_t, state_a_t, state_c_t, p, big):
    n_pos, n_seq, _ = x_t.shape
    sb = SEQ_BLK_MIX

    def by_pos(width):
        return pl.BlockSpec((n_pos, sb, width), lambda i: (0, i, 0))

    layer_slice = functools.partial(_resident, layer=layer)

    return pl.pallas_call(
        _sample_mixer_kernel,
        grid=(n_seq // sb,),
        in_specs=[
            by_pos(D_MODEL),
            by_pos(W),
            pl.BlockSpec((1, CONV_A_W - 1, sb, W), lambda i: (layer, 0, i, 0)),
            pl.BlockSpec((1, CONV_C_W - 1, sb, W), lambda i: (layer, 0, i, 0)),
            _resident((D_MODEL, IN_COLS), 0),
            layer_slice((CONV_A_W, W)),
            layer_slice((1, W)),
            layer_slice((1, W)),
            layer_slice((n_pos * n_pos, W)),
            layer_slice((n_pos, W)),
            layer_slice((CONV_C_W, W)),
            layer_slice((1, W)),
            layer_slice((1, W)),
            layer_slice((1, W)),
            _resident((N_BRANCH, W, D_MODEL), 0),
            _resident((D_MODEL, D_MODEL), 0),
            layer_slice((1, D_MODEL)),
            layer_slice((1, D_MODEL)),
        ],
        out_specs=[
            by_pos(D_MODEL),
            by_pos(W),
            pl.BlockSpec((CONV_C_W - 1, sb, W), lambda i: (0, i, 0)),
            by_pos(W),
        ],
        out_shape=[
            jax.ShapeDtypeStruct((n_pos, n_seq, D_MODEL), F32),
            jax.ShapeDtypeStruct((n_pos, n_seq, W), F32),
            jax.ShapeDtypeStruct((CONV_C_W - 1, n_seq, W), F32),
            jax.ShapeDtypeStruct((n_pos, n_seq, W), F32),
        ],
        compiler_params=pltpu.CompilerParams(
            dimension_semantics=("arbitrary",), vmem_limit_bytes=VMEM_LIMIT),
        name="sample_mixer",
    )(x_t, yx_t, state_a_t, state_c_t, big["w_in"], p["w_conv_a"], p["ln_v_g"], p["ln_v_b"],
      p["w_s_rows"], p["b_s_rows"], p["w_conv_c"], p["b_conv_c"], p["ln_c_g"], p["ln_c_b"],
      big["w_out_br"], big["w_o"], p["ln1_g"], p["ln1_b"])


def _mlp_kernel(x_ref, wu_ref, bu_ref, wd_ref, bd_ref, g_ref, b_ref, *rest):
    n_side = (len(rest) - 1) // 2
    y_ref = rest[n_side]
    for src, dst in zip(rest[:n_side], rest[n_side + 1:]):
        dst[...] = src[...].astype(BF16)
    x = x_ref[...]
    h = jnp.maximum(_dot(x.astype(BF16), wu_ref[0]) + bu_ref[0], 0.0)
    h = _dot((h * h).astype(BF16), wd_ref[0]) + bd_ref[0]
    y_ref[...] = _layer_norm(ALPHA * x + h, g_ref[0], b_ref[0])


def _mlp(layer, x2d, p, wu16, wd16, round_next=()):
    rows = x2d.shape[0]
    tm = min(TM_MLP, rows)
    n_steps = rows // tm
    side_in, side_out, side_shape = [], [], []
    for w32, l_next in round_next:
        _, r, c = w32.shape
        slab = r // n_steps
        assert slab * n_steps == r and slab % BF16_SUBLANES == 0
        side_in.append(pl.BlockSpec((1, slab, c), functools.partial(lambda i, l: (l, i, 0), l=l_next)))
        side_out.append(pl.BlockSpec((1, slab, c), lambda i: (0, i, 0)))
        side_shape.append(jax.ShapeDtypeStruct((1, r, c), BF16))
    out = pl.pallas_call(
        _mlp_kernel,
        grid=(n_steps,),
        in_specs=[
            pl.BlockSpec((tm, D_MODEL), lambda i: (i, 0)),
            _resident((D_MODEL, D_FF), 0),
            _resident((1, D_FF), layer),
            _resident((D_FF, D_MODEL), 0),
            _resident((1, D_MODEL), layer),
            _resident((1, D_MODEL), layer),
            _resident((1, D_MODEL), layer),
        ] + side_in,
        out_specs=[pl.BlockSpec((tm, D_MODEL), lambda i: (i, 0))] + side_out,
        out_shape=[jax.ShapeDtypeStruct((rows, D_MODEL), F32)] + side_shape,
        compiler_params=pltpu.CompilerParams(
            dimension_semantics=("arbitrary",), vmem_limit_bytes=VMEM_LIMIT),
        name="mlp",
    )(x2d, wu16, p["b_up"], wd16, p["b_down"], p["ln2_g"], p["ln2_b"], *[w for w, _ in round_next])
    return out[0], out[1:]


def kernel(x_prompt, x_sample, mem_prompt, state_conv_a, state_conv_c, cache_mem_k, cache_mem_v,
           w_in, w_conv_a, ln_v_g, ln_v_b, w_s, b_s, w_conv_c, b_conv_c, ln_c_g, ln_c_b,
           w_mem_kv, w_out_br, w_o, ln1_g, ln1_b, w_up, b_up, w_down, b_down, ln2_g, ln2_b):
    bp, seq, _ = x_prompt.shape
    n_seq, n_pos, _ = x_sample.shape
    assert seq % TM_MIX == 0 and TM_MIX % CHUNK == 0 and TM_MIX % CONV_RB == 0
    assert n_seq % SEQ_BLK == 0 and n_seq % SEQ_BLK_MIX == 0 and n_pos <= CHUNK and n_pos >= CONV_A_W - 1

    row = lambda a: a[:, None, :]
    params = {
        "w_conv_a": w_conv_a, "ln_v_g": row(ln_v_g), "ln_v_b": row(ln_v_b),
        "w_s": w_s, "b_s_t": jnp.swapaxes(b_s, 1, 2),
        "w_s_rows": jnp.repeat(
            jnp.transpose(w_s[:, :, :n_pos, :n_pos], (0, 2, 3, 1)).reshape(DEPTH, n_pos * n_pos, GROUPS),
            GROUP_W, axis=-1),
        "b_s_rows": jnp.repeat(jnp.swapaxes(b_s[:, :, :n_pos], 1, 2), GROUP_W, axis=-1),
        "w_conv_c": w_conv_c, "b_conv_c": row(b_conv_c), "ln_c_g": row(ln_c_g), "ln_c_b": row(ln_c_b),
        "ln1_g": row(ln1_g), "ln1_b": row(ln1_b),
        "b_up": row(b_up), "b_down": row(b_down), "ln2_g": row(ln2_g), "ln2_b": row(ln2_b),
    }
    w_br2d = w_out_br.reshape(DEPTH, N_BRANCH * W, D_MODEL)
    big = {"w_in": w_in[0:1].astype(BF16), "w_out_br": w_out_br[0:1].astype(BF16),
           "w_o": w_o[0:1].astype(BF16)}

    pk, pv = _kv_project(mem_prompt.reshape(bp * N_MEM, D_MODEL), w_mem_kv)
    pk = pk.reshape(DEPTH, bp, N_MEM * HEADS, HEAD_DIM)
    pv = pv.reshape(DEPTH, bp, N_MEM * HEADS, HEAD_DIM)

    ys = jnp.swapaxes(x_sample, 0, 1)
    state_a_t = jnp.swapaxes(state_conv_a, 1, 2)
    state_c_t = jnp.swapaxes(state_conv_c, 1, 2)
    cache_k = cache_mem_k.reshape(DEPTH, n_seq, N_MEM * HEADS, HEAD_DIM)
    cache_v = cache_mem_v.reshape(DEPTH, n_seq, N_MEM * HEADS, HEAD_DIM)

    yp = x_prompt
    pa, pc, sa, sc, sv = [], [], [], [], []
    for l in range(DEPTH):
        x1, nba, nbc, wu16, wd16 = _prompt_mixer(l, yp, pk, pv, params, big, w_up, w_down)
        nxt = [(w_in, l + 1), (w_br2d, l + 1), (w_o, l + 1)] if l + 1 < DEPTH else []
        yp, rounded = _mlp(l, x1.reshape(bp * seq, D_MODEL), params, wu16, wd16, nxt)
        yp = yp.reshape(bp, seq, D_MODEL)
        pa.append(nba)
        pc.append(nbc)

        yx = _sample_attn(l, ys, big["w_in"], cache_k, cache_v)
        s1, p_t, nsc_t, v_t = _sample_mixer(l, ys, yx, state_a_t, state_c_t, params, big)
        ys, _ = _mlp(l, s1.reshape(n_pos * n_seq, D_MODEL), params, wu16, wd16)
        ys = ys.reshape(n_pos, n_seq, D_MODEL)
        sa.append(jnp.swapaxes(p_t[n_pos - (CONV_A_W - 1):], 0, 1))
        sc.append(nsc_t)
        sv.append(jnp.swapaxes(v_t, 0, 1))
        if rounded:
            big = {"w_in": rounded[0], "w_out_br": rounded[1].reshape(1, N_BRANCH, W, D_MODEL),
                   "w_o": rounded[2]}

    return (yp, jnp.swapaxes(ys, 0, 1), jnp.stack(pa), jnp.stack(pc),
            pk.reshape(DEPTH, bp, N_MEM, HEADS, HEAD_DIM), pv.reshape(DEPTH, bp, N_MEM, HEADS, HEAD_DIM),
            jnp.stack(sa), jnp.swapaxes(jnp.stack(sc), 1, 2), jnp.stack(sv))
```

```python
import functools
import math

import jax
import jax.numpy as jnp
from jax import lax
from jax.experimental import pallas as pl
from jax.experimental.pallas import tpu as pltpu

D_MODEL = 1024
DEPTH = 4
N_MEM = 256
W = D_MODEL // 2
CONV_A_W = 3
CHUNK = 128
GROUPS = 4
GROUP_W = W // GROUPS
CONV_C_W = 31
HEADS = 4
HEAD_DIM = W // HEADS
D_FF = 4 * D_MODEL
N_BRANCH = 4
IN_COLS = 8 * W + N_BRANCH * D_MODEL
ALPHA = (2 * DEPTH) ** 0.25
LN_EPS = 1e-5
ATTN_SCALE = HEAD_DIM ** -0.5

V7X_VMEM_BYTES = 64 * 1024 * 1024
VMEM_LIMIT = V7X_VMEM_BYTES - 8 * 1024 * 1024
SUBLANES = 8
BF16_SUBLANES = 16
LANES = 128

TM_MIX = 512
TM_MLP = 512
HALO_A = 8
HALO_C = 32
CONV_RB = 64
SEQ_BLK = 8
SEQ_BLK_MIX = 32

F32 = jnp.float32
BF16 = jnp.bfloat16


def _layer_norm(x, g, b):
    mu = jnp.mean(x, axis=-1, keepdims=True)
    xc = x - mu
    var = jnp.mean(xc * xc, axis=-1, keepdims=True)
    return xc * lax.rsqrt(var + LN_EPS) * g + b


def _dot(a, b):
    return jnp.dot(a, b, preferred_element_type=F32)


def _dot_nt(a, b):
    return lax.dot_general(a, b, (((1,), (1,)), ((), ())), preferred_element_type=F32)


def _gelu_tanh(x):
    c2 = 2.0 * math.sqrt(2.0 / math.pi)
    return x * jax.nn.sigmoid(x * (c2 + (c2 * 0.044715) * (x * x)))


def _softmax_rows(s):
    m = jnp.max(s, axis=-1, keepdims=True)
    e = jnp.exp(s - m)
    return e / jnp.sum(e, axis=-1, keepdims=True)


def _merge_and_norm(x, gate, ys, wbr_ref, wo_ref, ln1g_ref, ln1b_ref):
    acc = None
    for n in range(N_BRANCH):
        term = gate(n) * _dot(ys[n].astype(BF16), wbr_ref[0, n])
        acc = term if acc is None else acc + term
    mix = _dot(acc.astype(BF16), wo_ref[0])
    return _layer_norm(ALPHA * x + mix, ln1g_ref[0], ln1b_ref[0])


def _kv_kernel(mem_ref, w_ref, k_ref, v_ref):
    rows = mem_ref.shape[0]
    kv = _dot(mem_ref[...].astype(BF16), w_ref[0].astype(BF16))
    for h in range(HEADS):
        k_ref[0, pl.ds(h, rows, stride=HEADS), :] = kv[:, h * HEAD_DIM:(h + 1) * HEAD_DIM]
        v_ref[0, pl.ds(h, rows, stride=HEADS), :] = kv[:, W + h * HEAD_DIM:W + (h + 1) * HEAD_DIM]


def _kv_project(mem2d, w_kv):
    rows = mem2d.shape[0]
    return pl.pallas_call(
        _kv_kernel,
        grid=(DEPTH,),
        in_specs=[
            pl.BlockSpec((rows, D_MODEL), lambda l: (0, 0)),
            pl.BlockSpec((1, D_MODEL, 2 * W), lambda l: (l, 0, 0)),
        ],
        out_specs=[
            pl.BlockSpec((1, rows * HEADS, HEAD_DIM), lambda l: (l, 0, 0)),
            pl.BlockSpec((1, rows * HEADS, HEAD_DIM), lambda l: (l, 0, 0)),
        ],
        out_shape=[jax.ShapeDtypeStruct((DEPTH, rows * HEADS, HEAD_DIM), F32)] * 2,
        compiler_params=pltpu.CompilerParams(
            dimension_semantics=("arbitrary",), vmem_limit_bytes=VMEM_LIMIT),
        name="kv_project",
    )(mem2d, w_kv)


def _conv_c_rows(ext_c, wb_ref, r0):
    lead = HALO_C - (CONV_C_W - 1)
    span = CONV_RB + HALO_C
    cols = []
    for c in range(W // LANES):
        lanes = slice(c * LANES, (c + 1) * LANES)
        win = ext_c[r0:r0 + span, lanes]
        acc = None
        for b in range(SUBLANES):
            sh = win if b == 0 else pltpu.roll(win, span - b, axis=0)
            for a in range(HALO_C // SUBLANES + 1):
                k = a * SUBLANES + b - lead
                if 0 <= k < CONV_C_W:
                    tap = sh[a * SUBLANES:a * SUBLANES + CONV_RB].reshape(CONV_RB // SUBLANES, SUBLANES, LANES)
                    term = wb_ref[k * SUBLANES:(k + 1) * SUBLANES, lanes][None] * tap
                    acc = term if acc is None else acc + term
        cols.append(acc.reshape(CONV_RB, LANES))
    return jnp.concatenate(cols, axis=1)


def _mixer_stage(z_ref, xb, x_ref, mk_ref, mv_ref, w_in_ref, wca_ref, lnvg_ref, lnvb_ref, ws_ref, bst_ref,
                 bcc_ref, lncg_ref, lncb_ref, wbr_ref, wo_ref, ln1g_ref, ln1b_ref,
                 y_ref, nba_ref, nbc_ref, ext_a, ext_c, wb_c):
    tm = z_ref.shape[0]
    n_chunks = tm // CHUNK

    glu = z_ref[:, 5 * W:6 * W] * z_ref[:, 6 * W:7 * W]
    ext_c[HALO_C:HALO_C + tm, :] = glu
    conv_c = jnp.concatenate(
        [_conv_c_rows(ext_c, wb_c, rb * CONV_RB) for rb in range(tm // CONV_RB)], axis=0) + bcc_ref[0]
    nbc_ref[0] = ext_c[HALO_C + tm - (CONV_C_W - 1):HALO_C + tm, :]
    ext_c[0:HALO_C, :] = ext_c[tm:tm + HALO_C, :]
    y_c = jax.nn.silu(_layer_norm(conv_c, lncg_ref[0], lncb_ref[0]))

    p = z_ref[:, 2 * W:3 * W] * z_ref[:, 0:W]
    ext_a[HALO_A:HALO_A + tm, :] = p
    wca = wca_ref[0]
    conv_a = (wca[0:1, :] * ext_a[HALO_A - 2:HALO_A - 2 + tm, :]
              + wca[1:2, :] * ext_a[HALO_A - 1:HALO_A - 1 + tm, :]
              + wca[2:3, :] * p)
    y_a = z_ref[:, W:2 * W] * conv_a
    nba_ref[0] = ext_a[HALO_A + tm - 2:HALO_A + tm, :]
    ext_a[0:HALO_A, :] = ext_a[tm:tm + HALO_A, :]

    u = z_ref[:, 3 * W:4 * W]
    v = _layer_norm(z_ref[:, 4 * W:5 * W], lnvg_ref[0], lnvb_ref[0]).astype(BF16)
    row = lax.broadcasted_iota(jnp.int32, (CHUNK, CHUNK), 0)
    col = lax.broadcasted_iota(jnp.int32, (CHUNK, CHUNK), 1)
    causal = row >= col
    gate_cols = []
    for g in range(GROUPS):
        wg = jnp.where(causal, ws_ref[0, g], 0.0).astype(BF16)
        rhs = jnp.concatenate(
            [v[n * CHUNK:(n + 1) * CHUNK, g * GROUP_W:(g + 1) * GROUP_W] for n in range(n_chunks)],
            axis=1)
        gate_cols.append(_dot(wg, rhs) + bst_ref[0, :, g:g + 1])
    s = jnp.concatenate(
        [jnp.concatenate([gate_cols[g][:, n * GROUP_W:(n + 1) * GROUP_W] for g in range(GROUPS)], axis=1)
         for n in range(n_chunks)], axis=0)
    y_b = u * s

    heads = []
    for h in range(HEADS):
        qh = z_ref[:, 7 * W + h * HEAD_DIM:7 * W + (h + 1) * HEAD_DIM].astype(BF16)
        kh = mk_ref[0, 0, pl.ds(h, N_MEM, stride=HEADS), :].astype(BF16)
        vh = mv_ref[0, 0, pl.ds(h, N_MEM, stride=HEADS), :].astype(BF16)
        sc = _dot_nt(qh, kh) * ATTN_SCALE
        pr = _softmax_rows(sc).astype(BF16)
        heads.append(_dot(pr, vh))
    y_x = jnp.concatenate(heads, axis=1)

    gate = lambda n: jax.nn.sigmoid(
        _dot(xb, w_in_ref[0, :, 8 * W + n * D_MODEL:8 * W + (n + 1) * D_MODEL]))
    y_ref[0] = _merge_and_norm(x_ref[0], gate, (y_a, y_b, y_c, y_x), wbr_ref, wo_ref,
                               ln1g_ref, ln1b_ref)


def _prompt_mixer_kernel(x_ref, mk_ref, mv_ref, w_in_ref, wca_ref, lnvg_ref, lnvb_ref,
                         ws_ref, bst_ref, wcc_ref, bcc_ref, lncg_ref, lncb_ref, wbr_ref, wo_ref,
                         ln1g_ref, ln1b_ref, wu32_ref, wd32_ref,
                         y_ref, nba_ref, nbc_ref, wu16_ref, wd16_ref, z_ref, ext_a, ext_c, wb_c):
    wu16_ref[...] = wu32_ref[...].astype(BF16)
    wd16_ref[...] = wd32_ref[...].astype(BF16)

    @pl.when(pl.program_id(1) == 0)
    def _():
        ext_a[0:HALO_A, :] = jnp.zeros((HALO_A, W), F32)
        ext_c[0:HALO_C, :] = jnp.zeros((HALO_C, W), F32)
        for k in range(CONV_C_W):
            wb_c[k * SUBLANES:(k + 1) * SUBLANES, :] = jnp.broadcast_to(wcc_ref[0, k:k + 1, :], (SUBLANES, W))

    x = x_ref[0]
    xb = x.astype(BF16)
    identity = lambda a: a
    for c0, c1, act in ((5 * W, 6 * W, identity), (6 * W, 7 * W, jax.nn.sigmoid), (0, 3 * W, identity),
                        (3 * W, 5 * W, _gelu_tanh), (7 * W, 8 * W, identity)):
        z_ref[:, c0:c1] = act(_dot(xb, w_in_ref[0, :, c0:c1]))
    _mixer_stage(z_ref, xb, x_ref, mk_ref, mv_ref, w_in_ref, wca_ref, lnvg_ref, lnvb_ref, ws_ref,
                 bst_ref, bcc_ref, lncg_ref, lncb_ref, wbr_ref, wo_ref, ln1g_ref, ln1b_ref,
                 y_ref, nba_ref, nbc_ref, ext_a, ext_c, wb_c)


def _resident(shape_tail, layer):
    nd = len(shape_tail)
    return pl.BlockSpec((1,) + tuple(shape_tail), lambda *_: (layer,) + (0,) * nd,
                        pipeline_mode=pl.Buffered(1))


def _prompt_mixer(layer, x, mk, mv, p, big, w_up, w_down):
    bn, seq, _ = x.shape
    tm = TM_MIX
    tiles = seq // tm
    n_steps = bn * tiles
    up_rows, down_rows = D_MODEL // n_steps, D_FF // n_steps
    assert up_rows * n_steps == D_MODEL and up_rows % BF16_SUBLANES == 0
    return pl.pallas_call(
        _prompt_mixer_kernel,
        grid=(bn, tiles),
        in_specs=[
            pl.BlockSpec((1, tm, D_MODEL), lambda b, j: (b, j, 0)),
            pl.BlockSpec((1, 1, N_MEM * HEADS, HEAD_DIM), lambda b, j: (layer, b, 0, 0)),
            pl.BlockSpec((1, 1, N_MEM * HEADS, HEAD_DIM), lambda b, j: (layer, b, 0, 0)),
            _resident((D_MODEL, IN_COLS), 0),
            _resident((CONV_A_W, W), layer),
            _resident((1, W), layer),
            _resident((1, W), layer),
            _resident((GROUPS, CHUNK, CHUNK), layer),
            _resident((CHUNK, GROUPS), layer),
            _resident((CONV_C_W, W), layer),
            _resident((1, W), layer),
            _resident((1, W), layer),
            _resident((1, W), layer),
            _resident((N_BRANCH, W, D_MODEL), 0),
            _resident((D_MODEL, D_MODEL), 0),
            _resident((1, D_MODEL), layer),
            _resident((1, D_MODEL), layer),
            pl.BlockSpec((1, up_rows, D_FF), lambda b, j: (layer, b * tiles + j, 0)),
            pl.BlockSpec((1, down_rows, D_MODEL), lambda b, j: (layer, b * tiles + j, 0)),
        ],
        out_specs=[
            pl.BlockSpec((1, tm, D_MODEL), lambda b, j: (b, j, 0)),
            pl.BlockSpec((1, CONV_A_W - 1, W), lambda b, j: (b, 0, 0)),
            pl.BlockSpec((1, CONV_C_W - 1, W), lambda b, j: (b, 0, 0)),
            pl.BlockSpec((1, up_rows, D_FF), lambda b, j: (0, b * tiles + j, 0)),
            pl.BlockSpec((1, down_rows, D_MODEL), lambda b, j: (0, b * tiles + j, 0)),
        ],
        out_shape=[
            jax.ShapeDtypeStruct((bn, seq, D_MODEL), F32),
            jax.ShapeDtypeStruct((bn, CONV_A_W - 1, W), F32),
            jax.ShapeDtypeStruct((bn, CONV_C_W - 1, W), F32),
            jax.ShapeDtypeStruct((1, D_MODEL, D_FF), BF16),
            jax.ShapeDtypeStruct((1, D_FF, D_MODEL), BF16),
        ],
        scratch_shapes=[
            pltpu.VMEM((tm, 8 * W), F32),
            pltpu.VMEM((tm + HALO_A, W), F32),
            pltpu.VMEM((tm + HALO_C, W), F32),
            pltpu.VMEM((CONV_C_W * SUBLANES, W), F32),
        ],
        compiler_params=pltpu.CompilerParams(
            dimension_semantics=("arbitrary", "arbitrary"), vmem_limit_bytes=VMEM_LIMIT),
        name="prompt_mixer",
    )(x, mk, mv, big["w_in"], p["w_conv_a"], p["ln_v_g"], p["ln_v_b"], p["w_s"], p["b_s_t"],
      p["w_conv_c"], p["b_conv_c"], p["ln_c_g"], p["ln_c_b"], big["w_out_br"], big["w_o"],
      p["ln1_g"], p["ln1_b"], w_up, w_down)


def _sample_attn_kernel(x_ref, wq_ref, k_ref, v_ref, o_ref):
    n_pos, sb, _ = x_ref.shape
    rows = n_pos * sb
    xb = x_ref[...].reshape(rows, D_MODEL).astype(BF16)
    q = _dot(xb, wq_ref[0])
    r_seq = lax.broadcasted_iota(jnp.int32, (rows, sb * N_MEM), 0) % sb
    c_seq = lax.broadcasted_iota(jnp.int32, (rows, sb * N_MEM), 1) // N_MEM
    own = r_seq == c_seq
    heads = []
    for h in range(HEADS):
        sl = slice(h * HEAD_DIM, (h + 1) * HEAD_DIM)
        kh = k_ref[:, pl.ds(h, N_MEM, stride=HEADS), :].reshape(sb * N_MEM, HEAD_DIM).astype(BF16)
        vh = v_ref[:, pl.ds(h, N_MEM, stride=HEADS), :].reshape(sb * N_MEM, HEAD_DIM).astype(BF16)
        sc = _dot_nt(q[:, sl].astype(BF16), kh) * ATTN_SCALE
        sc = jnp.where(own, sc, -jnp.inf)
        pr = _softmax_rows(sc).astype(BF16)
        heads.append(_dot(pr, vh))
    o_ref[...] = jnp.concatenate(heads, axis=1).reshape(n_pos, sb, W)


def _sample_attn(layer, x_t, w_in_bf, cache_k, cache_v):
    n_pos, n_seq, _ = x_t.shape
    sb = SEQ_BLK
    q_block = (7 * W) // W
    return pl.pallas_call(
        _sample_attn_kernel,
        grid=(n_seq // sb,),
        in_specs=[
            pl.BlockSpec((n_pos, sb, D_MODEL), lambda i: (0, i, 0)),
            pl.BlockSpec((1, D_MODEL, W), lambda i: (0, 0, q_block)),
            pl.BlockSpec((None, sb, N_MEM * HEADS, HEAD_DIM), lambda i: (layer, i, 0, 0)),
            pl.BlockSpec((None, sb, N_MEM * HEADS, HEAD_DIM), lambda i: (layer, i, 0, 0)),
        ],
        out_specs=pl.BlockSpec((n_pos, sb, W), lambda i: (0, i, 0)),
        out_shape=jax.ShapeDtypeStruct((n_pos, n_seq, W), F32),
        compiler_params=pltpu.CompilerParams(
            dimension_semantics=("arbitrary",), vmem_limit_bytes=VMEM_LIMIT),
        name="sample_attn",
    )(x_t, w_in_bf, cache_k, cache_v)


def _sample_mixer_kernel(x_ref, yx_ref, sa_ref, sc_ref, w_in_ref, wca_ref, lnvg_ref, lnvb_ref,
                         wsr_ref, bsr_ref, wcc_ref, bcc_ref, lncg_ref, lncb_ref, wbr_ref, wo_ref,
                         ln1g_ref, ln1b_ref,
                         y_ref, p_ref, nsc_ref, v_ref):
    n_pos, n_seq, _ = x_ref.shape
    rows = n_pos * n_seq
    x = x_ref[...].reshape(rows, D_MODEL)
    xb = x.astype(BF16)

    def proj(c0, c1):
        return _dot(xb, w_in_ref[0, :, c0:c1])

    def pos(a, t):
        return a[t * n_seq:(t + 1) * n_seq, :]

    za = proj(0, 3 * W)
    p = za[:, 2 * W:3 * W] * za[:, 0:W]
    p_ref[...] = p.reshape(n_pos, n_seq, W)
    wca = wca_ref[0]
    hist_a = [sa_ref[0, r] for r in range(CONV_A_W - 1)] + [pos(p, t) for t in range(n_pos)]
    conv_a = jnp.concatenate(
        [sum(wca[k:k + 1, :] * hist_a[t + k] for k in range(CONV_A_W)) for t in range(n_pos)], axis=0)
    y_a = za[:, W:2 * W] * conv_a

    zb = _gelu_tanh(proj(3 * W, 5 * W))
    u = zb[:, :W]
    v = _layer_norm(zb[:, W:], lnvg_ref[0], lnvb_ref[0])
    v_ref[...] = v.reshape(n_pos, n_seq, W)
    gate_rows = []
    for t in range(n_pos):
        acc = bsr_ref[0, t:t + 1, :]
        for s_ in range(t + 1):
            acc = acc + wsr_ref[0, t * n_pos + s_:t * n_pos + s_ + 1, :] * pos(v, s_)
        gate_rows.append(acc)
    y_b = u * jnp.concatenate(gate_rows, axis=0)

    zc = proj(5 * W, 7 * W)
    glu = zc[:, :W] * jax.nn.sigmoid(zc[:, W:])
    keep = CONV_C_W - 1 - n_pos
    nsc_ref[0:keep] = sc_ref[0, n_pos:CONV_C_W - 1]
    nsc_ref[keep:CONV_C_W - 1] = glu.reshape(n_pos, n_seq, W)
    wcc = wcc_ref[0]
    conv_rows = []
    for t in range(n_pos):
        acc = None
        for k in range(CONV_C_W):
            r = t + k
            src = sc_ref[0, r] if r < CONV_C_W - 1 else pos(glu, r - (CONV_C_W - 1))
            term = wcc[k:k + 1, :] * src
            acc = term if acc is None else acc + term
        conv_rows.append(acc)
    conv_c = jnp.concatenate(conv_rows, axis=0) + bcc_ref[0]
    y_c = jax.nn.silu(_layer_norm(conv_c, lncg_ref[0], lncb_ref[0]))

    y_x = yx_ref[...].reshape(rows, W)
    gate = lambda n: jax.nn.sigmoid(proj(8 * W + n * D_MODEL, 8 * W + (n + 1) * D_MODEL))
    y = _merge_and_norm(x, gate, (y_a, y_b, y_c, y_x), wbr_ref, wo_ref, ln1g_ref, ln1b_ref)
    y_ref[...] = y.reshape(n_pos, n_seq, D_MODEL)


def _sample_mixer(layer, x_t, yx_t, state_a_t, state_c_t, p, big):
    n_pos, n_seq, _ = x_t.shape
    sb = SEQ_BLK_MIX

    def by_pos(width):
        return pl.BlockSpec((n_pos, sb, width), lambda i: (0, i, 0))

    layer_slice = functools.partial(_resident, layer=layer)

    return pl.pallas_call(
        _sample_mixer_kernel,
        grid=(n_seq // sb,),
        in_specs=[
            by_pos(D_MODEL),
            by_pos(W),
            pl.BlockSpec((1, CONV_A_W - 1, sb, W), lambda i: (layer, 0, i, 0)),
            pl.BlockSpec((1, CONV_C_W - 1, sb, W), lambda i: (layer, 0, i, 0)),
            _resident((D_MODEL, IN_COLS), 0),
            layer_slice((CONV_A_W, W)),
            layer_slice((1, W)),
            layer_slice((1, W)),
            layer_slice((n_pos * n_pos, W)),
            layer_slice((n_pos, W)),
            layer_slice((CONV_C_W, W)),
            layer_slice((1, W)),
            layer_slice((1, W)),
            layer_slice((1, W)),
            _resident((N_BRANCH, W, D_MODEL), 0),
            _resident((D_MODEL, D_MODEL), 0),
            layer_slice((1, D_MODEL)),
            layer_slice((1, D_MODEL)),
        ],
        out_specs=[
            by_pos(D_MODEL),
            by_pos(W),
            pl.BlockSpec((CONV_C_W - 1, sb, W), lambda i: (0, i, 0)),
            by_pos(W),
        ],
        out_shape=[
            jax.ShapeDtypeStruct((n_pos, n_seq, D_MODEL), F32),
            jax.ShapeDtypeStruct((n_pos, n_seq, W), F32),
            jax.ShapeDtypeStruct((CONV_C_W - 1, n_seq, W), F32),
            jax.ShapeDtypeStruct((n_pos, n_seq, W), F32),
        ],
        compiler_params=pltpu.CompilerParams(
            dimension_semantics=("arbitrary",), vmem_limit_bytes=VMEM_LIMIT),
        name="sample_mixer",
    )(x_t, yx_t, state_a_t, state_c_t, big["w_in"], p["w_conv_a"], p["ln_v_g"], p["ln_v_b"],
      p["w_s_rows"], p["b_s_rows"], p["w_conv_c"], p["b_conv_c"], p["ln_c_g"], p["ln_c_b"],
      big["w_out_br"], big["w_o"], p["ln1_g"], p["ln1_b"])


def _mlp_kernel(x_ref, wu_ref, bu_ref, wd_ref, bd_ref, g_ref, b_ref, *rest):
    n_side = (len(rest) - 1) // 2
    y_ref = rest[n_side]
    for src, dst in zip(rest[:n_side], rest[n_side + 1:]):
        dst[...] = src[...].astype(BF16)
    x = x_ref[...]
    h = jnp.maximum(_dot(x.astype(BF16), wu_ref[0]) + bu_ref[0], 0.0)
    h = _dot((h * h).astype(BF16), wd_ref[0]) + bd_ref[0]
    y_ref[...] = _layer_norm(ALPHA * x + h, g_ref[0], b_ref[0])


def _mlp(layer, x2d, p, wu16, wd16, round_next=()):
    rows = x2d.shape[0]
    tm = min(TM_MLP, rows)
    n_steps = rows // tm
    side_in, side_out, side_shape = [], [], []
    for w32, l_next in round_next:
        _, r, c = w32.shape
        slab = r // n_steps
        assert slab * n_steps == r and slab % BF16_SUBLANES == 0
        side_in.append(pl.BlockSpec((1, slab, c), functools.partial(lambda i, l: (l, i, 0), l=l_next)))
        side_out.append(pl.BlockSpec((1, slab, c), lambda i: (0, i, 0)))
        side_shape.append(jax.ShapeDtypeStruct((1, r, c), BF16))
    out = pl.pallas_call(
        _mlp_kernel,
        grid=(n_steps,),
        in_specs=[
            pl.BlockSpec((tm, D_MODEL), lambda i: (i, 0)),
            _resident((D_MODEL, D_FF), 0),
            _resident((1, D_FF), layer),
            _resident((D_FF, D_MODEL), 0),
            _resident((1, D_MODEL), layer),
            _resident((1, D_MODEL), layer),
            _resident((1, D_MODEL), layer),
        ] + side_in,
        out_specs=[pl.BlockSpec((tm, D_MODEL), lambda i: (i, 0))] + side_out,
        out_shape=[jax.ShapeDtypeStruct((rows, D_MODEL), F32)] + side_shape,
        compiler_params=pltpu.CompilerParams(
            dimension_semantics=("arbitrary",), vmem_limit_bytes=VMEM_LIMIT),
        name="mlp",
    )(x2d, wu16, p["b_up"], wd16, p["b_down"], p["ln2_g"], p["ln2_b"], *[w for w, _ in round_next])
    return out[0], out[1:]


def kernel(x_prompt, x_sample, mem_prompt, state_conv_a, state_conv_c, cache_mem_k, cache_mem_v,
           w_in, w_conv_a, ln_v_g, ln_v_b, w_s, b_s, w_conv_c, b_conv_c, ln_c_g, ln_c_b,
           w_mem_kv, w_out_br, w_o, ln1_g, ln1_b, w_up, b_up, w_down, b_down, ln2_g, ln2_b):
    bp, seq, _ = x_prompt.shape
    n_seq, n_pos, _ = x_sample.shape
    assert seq % TM_MIX == 0 and TM_MIX % CHUNK == 0 and TM_MIX % CONV_RB == 0
    assert n_seq % SEQ_BLK == 0 and n_seq % SEQ_BLK_MIX == 0 and n_pos <= CHUNK and n_pos >= CONV_A_W - 1

    row = lambda a: a[:, None, :]
    params = {
        "w_conv_a": w_conv_a, "ln_v_g": row(ln_v_g), "ln_v_b": row(ln_v_b),
        "w_s": w_s, "b_s_t": jnp.swapaxes(b_s, 1, 2),
        "w_s_rows": jnp.repeat(
            jnp.transpose(w_s[:, :, :n_pos, :n_pos], (0, 2, 3, 1)).reshape(DEPTH, n_pos * n_pos, GROUPS),
            GROUP_W, axis=-1),
        "b_s_rows": jnp.repeat(jnp.swapaxes(b_s[:, :, :n_pos], 1, 2), GROUP_W, axis=-1),
        "w_conv_c": w_conv_c, "b_conv_c": row(b_conv_c), "ln_c_g": row(ln_c_g), "ln_c_b": row(ln_c_b),
        "ln1_g": row(ln1_g), "ln1_b": row(ln1_b),
        "b_up": row(b_up), "b_down": row(b_down), "ln2_g": row(ln2_g), "ln2_b": row(ln2_b),
    }
    w_br2d = w_out_br.reshape(DEPTH, N_BRANCH * W, D_MODEL)
    big = {"w_in": w_in[0:1].astype(BF16), "w_out_br": w_out_br[0:1].astype(BF16),
           "w_o": w_o[0:1].astype(BF16)}

    pk, pv = _kv_project(mem_prompt.reshape(bp * N_MEM, D_MODEL), w_mem_kv)
    pk = pk.reshape(DEPTH, bp, N_MEM * HEADS, HEAD_DIM)
    pv = pv.reshape(DEPTH, bp, N_MEM * HEADS, HEAD_DIM)

    ys = jnp.swapaxes(x_sample, 0, 1)
    state_a_t = jnp.swapaxes(state_conv_a, 1, 2)
    state_c_t = jnp.swapaxes(state_conv_c, 1, 2)
    cache_k = cache_mem_k.reshape(DEPTH, n_seq, N_MEM * HEADS, HEAD_DIM)
    cache_v = cache_mem_v.reshape(DEPTH, n_seq, N_MEM * HEADS, HEAD_DIM)

    yp = x_prompt
    pa, pc, sa, sc, sv = [], [], [], [], []
    for l in range(DEPTH):
        x1, nba, nbc, wu16, wd16 = _prompt_mixer(l, yp, pk, pv, params, big, w_up, w_down)
        nxt = [(w_in, l + 1), (w_br2d, l + 1), (w_o, l + 1)] if l + 1 < DEPTH else []
        yp, rounded = _mlp(l, x1.reshape(bp * seq, D_MODEL), params, wu16, wd16, nxt)
        yp = yp.reshape(bp, seq, D_MODEL)
        pa.append(nba)
        pc.append(nbc)

        yx = _sample_attn(l, ys, big["w_in"], cache_k, cache_v)
        s1, p_t, nsc_t, v_t = _sample_mixer(l, ys, yx, state_a_t, state_c_t, params, big)
        ys, _ = _mlp(l, s1.reshape(n_pos * n_seq, D_MODEL), params, wu16, wd16)
        ys = ys.reshape(n_pos, n_seq, D_MODEL)
        sa.append(jnp.swapaxes(p_t[n_pos - (CONV_A_W - 1):], 0, 1))
        sc.append(nsc_t)
        sv.append(jnp.swapaxes(v_t, 0, 1))
        if rounded:
            big = {"w_in": rounded[0], "w_out_br": rounded[1].reshape(1, N_BRANCH, W, D_MODEL),
                   "w_o": rounded[2]}

    return (yp, jnp.swapaxes(ys, 0, 1), jnp.stack(pa), jnp.stack(pc),
            pk.reshape(DEPTH, bp, N_MEM, HEADS, HEAD_DIM), pv.reshape(DEPTH, bp, N_MEM, HEADS, HEAD_DIM),
            jnp.stack(sa), jnp.swapaxes(jnp.stack(sc), 1, 2), jnp.stack(sv))
```

```python
import functools
import math

import jax
import jax.numpy as jnp
from jax import lax
from jax.experimental import pallas as pl
from jax.experimental.pallas import tpu as pltpu

D_MODEL = 1024
DEPTH = 4
N_MEM = 256
W = D_MODEL // 2
CONV_A_W = 3
CHUNK = 128
GROUPS = 4
GROUP_W = W // GROUPS
CONV_C_W = 31
HEADS = 4
HEAD_DIM = W // HEADS
D_FF = 4 * D_MODEL
N_BRANCH = 4
IN_COLS = 8 * W + N_BRANCH * D_MODEL
ALPHA = (2 * DEPTH) ** 0.25
LN_EPS = 1e-5
ATTN_SCALE = HEAD_DIM ** -0.5

V7X_VMEM_BYTES = 64 * 1024 * 1024
VMEM_LIMIT = V7X_VMEM_BYTES - 8 * 1024 * 1024
SUBLANES = 8
BF16_SUBLANES = 16
LANES = 128

TM_MIX = 256
TM_MLP = 512
HALO_A = 8
HALO_C = 32
CONV_RB = 64
SEQ_BLK = 8
SEQ_BLK_MIX = 32

F32 = jnp.float32
BF16 = jnp.bfloat16


def _layer_norm(x, g, b):
    mu = jnp.mean(x, axis=-1, keepdims=True)
    xc = x - mu
    var = jnp.mean(xc * xc, axis=-1, keepdims=True)
    return xc * lax.rsqrt(var + LN_EPS) * g + b


def _dot(a, b):
    return jnp.dot(a, b, preferred_element_type=F32)


def _dot_nt(a, b):
    return lax.dot_general(a, b, (((1,), (1,)), ((), ())), preferred_element_type=F32)


def _gelu_tanh(x):
    c1 = math.sqrt(2.0 / math.pi)
    u = x * (c1 + (c1 * 0.044715) * (x * x))
    h = 0.5 * x
    return h + h * jnp.tanh(u)


def _softmax_rows(s):
    m = jnp.max(s, axis=-1, keepdims=True)
    e = jnp.exp(s - m)
    return e / jnp.sum(e, axis=-1, keepdims=True)


def _merge_and_norm(x, gate_pre, ys, wbr_ref, wo_ref, ln1g_ref, ln1b_ref):
    acc = None
    for n in range(N_BRANCH):
        term = jax.nn.sigmoid(gate_pre(n)) * _dot(ys[n].astype(BF16), wbr_ref[0, n])
        acc = term if acc is None else acc + term
    mix = _dot(acc.astype(BF16), wo_ref[0])
    return _layer_norm(ALPHA * x + mix, ln1g_ref[0], ln1b_ref[0])


def _kv_kernel(mem_ref, w_ref, k_ref, v_ref):
    rows = mem_ref.shape[0]
    kv = _dot(mem_ref[...].astype(BF16), w_ref[0].astype(BF16))
    for h in range(HEADS):
        k_ref[0, pl.ds(h, rows, stride=HEADS), :] = kv[:, h * HEAD_DIM:(h + 1) * HEAD_DIM]
        v_ref[0, pl.ds(h, rows, stride=HEADS), :] = kv[:, W + h * HEAD_DIM:W + (h + 1) * HEAD_DIM]


def _kv_project(mem2d, w_kv):
    rows = mem2d.shape[0]
    return pl.pallas_call(
        _kv_kernel,
        grid=(DEPTH,),
        in_specs=[
            pl.BlockSpec((rows, D_MODEL), lambda l: (0, 0)),
            pl.BlockSpec((1, D_MODEL, 2 * W), lambda l: (l, 0, 0)),
        ],
        out_specs=[
            pl.BlockSpec((1, rows * HEADS, HEAD_DIM), lambda l: (l, 0, 0)),
            pl.BlockSpec((1, rows * HEADS, HEAD_DIM), lambda l: (l, 0, 0)),
        ],
        out_shape=[jax.ShapeDtypeStruct((DEPTH, rows * HEADS, HEAD_DIM), F32)] * 2,
        compiler_params=pltpu.CompilerParams(
            dimension_semantics=("arbitrary",), vmem_limit_bytes=VMEM_LIMIT),
        name="kv_project",
    )(mem2d, w_kv)


def _conv_c_rows(ext_c, wb_ref, r0):
    lead = HALO_C - (CONV_C_W - 1)
    span = CONV_RB + HALO_C
    cols = []
    for c in range(W // LANES):
        lanes = slice(c * LANES, (c + 1) * LANES)
        win = ext_c[r0:r0 + span, lanes]
        acc = None
        for b in range(SUBLANES):
            sh = win if b == 0 else pltpu.roll(win, span - b, axis=0)
            for a in range(HALO_C // SUBLANES + 1):
                k = a * SUBLANES + b - lead
                if 0 <= k < CONV_C_W:
                    tap = sh[a * SUBLANES:a * SUBLANES + CONV_RB].reshape(CONV_RB // SUBLANES, SUBLANES, LANES)
                    term = wb_ref[k * SUBLANES:(k + 1) * SUBLANES, lanes][None] * tap
                    acc = term if acc is None else acc + term
        cols.append(acc.reshape(CONV_RB, LANES))
    return jnp.concatenate(cols, axis=1)


def _mixer_stage(z_ref, x_ref, mk_ref, mv_ref, wca_ref, lnvg_ref, lnvb_ref, ws_ref, bst_ref,
                 bcc_ref, lncg_ref, lncb_ref, wbr_ref, wo_ref, ln1g_ref, ln1b_ref,
                 y_ref, nba_ref, nbc_ref, ext_a, ext_c, wb_c):
    tm = z_ref.shape[0]
    n_chunks = tm // CHUNK

    glu = z_ref[:, 5 * W:6 * W] * jax.nn.sigmoid(z_ref[:, 6 * W:7 * W])
    ext_c[HALO_C:HALO_C + tm, :] = glu
    conv_c = jnp.concatenate(
        [_conv_c_rows(ext_c, wb_c, rb * CONV_RB) for rb in range(tm // CONV_RB)], axis=0) + bcc_ref[0]
    nbc_ref[0] = ext_c[HALO_C + tm - (CONV_C_W - 1):HALO_C + tm, :]
    ext_c[0:HALO_C, :] = ext_c[tm:tm + HALO_C, :]
    y_c = jax.nn.silu(_layer_norm(conv_c, lncg_ref[0], lncb_ref[0]))

    p = z_ref[:, 2 * W:3 * W] * z_ref[:, 0:W]
    ext_a[HALO_A:HALO_A + tm, :] = p
    wca = wca_ref[0]
    conv_a = (wca[0:1, :] * ext_a[HALO_A - 2:HALO_A - 2 + tm, :]
              + wca[1:2, :] * ext_a[HALO_A - 1:HALO_A - 1 + tm, :]
              + wca[2:3, :] * p)
    y_a = z_ref[:, W:2 * W] * conv_a
    nba_ref[0] = ext_a[HALO_A + tm - 2:HALO_A + tm, :]
    ext_a[0:HALO_A, :] = ext_a[tm:tm + HALO_A, :]

    u = _gelu_tanh(z_ref[:, 3 * W:4 * W])
    v = _layer_norm(_gelu_tanh(z_ref[:, 4 * W:5 * W]), lnvg_ref[0], lnvb_ref[0]).astype(BF16)
    row = lax.broadcasted_iota(jnp.int32, (CHUNK, CHUNK), 0)
    col = lax.broadcasted_iota(jnp.int32, (CHUNK, CHUNK), 1)
    causal = row >= col
    gate_cols = []
    for g in range(GROUPS):
        wg = jnp.where(causal, ws_ref[0, g], 0.0).astype(BF16)
        rhs = jnp.concatenate(
            [v[n * CHUNK:(n + 1) * CHUNK, g * GROUP_W:(g + 1) * GROUP_W] for n in range(n_chunks)],
            axis=1)
        gate_cols.append(_dot(wg, rhs) + bst_ref[0, :, g:g + 1])
    s = jnp.concatenate(
        [jnp.concatenate([gate_cols[g][:, n * GROUP_W:(n + 1) * GROUP_W] for g in range(GROUPS)], axis=1)
         for n in range(n_chunks)], axis=0)
    y_b = u * s

    scores = []
    for h in range(HEADS):
        qh = z_ref[:, 7 * W + h * HEAD_DIM:7 * W + (h + 1) * HEAD_DIM].astype(BF16)
        kh = mk_ref[0, 0, pl.ds(h, N_MEM, stride=HEADS), :].astype(BF16)
        scores.append(_dot_nt(qh, kh))
    pr = _softmax_rows(jnp.concatenate(scores, axis=0) * ATTN_SCALE).astype(BF16)
    heads = []
    for h in range(HEADS):
        vh = mv_ref[0, 0, pl.ds(h, N_MEM, stride=HEADS), :].astype(BF16)
        heads.append(_dot(pr[h * tm:(h + 1) * tm], vh))
    y_x = jnp.concatenate(heads, axis=1)

    gate_pre = lambda n: z_ref[:, 8 * W + n * D_MODEL:8 * W + (n + 1) * D_MODEL]
    y_ref[0] = _merge_and_norm(x_ref[0], gate_pre, (y_a, y_b, y_c, y_x), wbr_ref, wo_ref,
                               ln1g_ref, ln1b_ref)


def _prompt_mixer_kernel(x_ref, mk_ref, mv_ref, w_in_ref, wca_ref, lnvg_ref, lnvb_ref,
                         ws_ref, bst_ref, wcc_ref, bcc_ref, lncg_ref, lncb_ref, wbr_ref, wo_ref,
                         ln1g_ref, ln1b_ref, wu32_ref, wd32_ref,
                         y_ref, nba_ref, nbc_ref, wu16_ref, wd16_ref, z_ref, ext_a, ext_c, wb_c):
    wu16_ref[...] = wu32_ref[...].astype(BF16)
    wd16_ref[...] = wd32_ref[...].astype(BF16)

    @pl.when(pl.program_id(1) == 0)
    def _():
        ext_a[0:HALO_A, :] = jnp.zeros((HALO_A, W), F32)
        ext_c[0:HALO_C, :] = jnp.zeros((HALO_C, W), F32)
        for k in range(CONV_C_W):
            wb_c[k * SUBLANES:(k + 1) * SUBLANES, :] = jnp.broadcast_to(wcc_ref[0, k:k + 1, :], (SUBLANES, W))

    x = x_ref[0]
    xb = x.astype(BF16)
    for c0, c1 in ((5 * W, 7 * W), (0, 3 * W), (3 * W, 5 * W), (7 * W, 8 * W)):
        z_ref[:, c0:c1] = _dot(xb, w_in_ref[0, :, c0:c1])
    for c0 in range(8 * W, IN_COLS, D_MODEL):
        z_ref[:, c0:c0 + D_MODEL] = _dot(xb, w_in_ref[0, :, c0:c0 + D_MODEL])
    _mixer_stage(z_ref, x_ref, mk_ref, mv_ref, wca_ref, lnvg_ref, lnvb_ref, ws_ref,
                 bst_ref, bcc_ref, lncg_ref, lncb_ref, wbr_ref, wo_ref, ln1g_ref, ln1b_ref,
                 y_ref, nba_ref, nbc_ref, ext_a, ext_c, wb_c)


def _resident(shape_tail, layer):
    nd = len(shape_tail)
    return pl.BlockSpec((1,) + tuple(shape_tail), lambda *_: (layer,) + (0,) * nd,
                        pipeline_mode=pl.Buffered(1))


def _prompt_mixer(layer, x, mk, mv, p, big, w_up, w_down):
    bn, seq, _ = x.shape
    tm = TM_MIX
    tiles = seq // tm
    n_steps = bn * tiles
    up_rows, down_rows = D_MODEL // n_steps, D_FF // n_steps
    assert up_rows * n_steps == D_MODEL and up_rows % BF16_SUBLANES == 0
    return pl.pallas_call(
        _prompt_mixer_kernel,
        grid=(bn, tiles),
        in_specs=[
            pl.BlockSpec((1, tm, D_MODEL), lambda b, j: (b, j, 0)),
            pl.BlockSpec((1, 1, N_MEM * HEADS, HEAD_DIM), lambda b, j: (layer, b, 0, 0)),
            pl.BlockSpec((1, 1, N_MEM * HEADS, HEAD_DIM), lambda b, j: (layer, b, 0, 0)),
            _resident((D_MODEL, IN_COLS), 0),
            _resident((CONV_A_W, W), layer),
            _resident((1, W), layer),
            _resident((1, W), layer),
            _resident((GROUPS, CHUNK, CHUNK), layer),
            _resident((CHUNK, GROUPS), layer),
            _resident((CONV_C_W, W), layer),
            _resident((1, W), layer),
            _resident((1, W), layer),
            _resident((1, W), layer),
            _resident((N_BRANCH, W, D_MODEL), 0),
            _resident((D_MODEL, D_MODEL), 0),
            _resident((1, D_MODEL), layer),
            _resident((1, D_MODEL), layer),
            pl.BlockSpec((1, up_rows, D_FF), lambda b, j: (layer, b * tiles + j, 0)),
            pl.BlockSpec((1, down_rows, D_MODEL), lambda b, j: (layer, b * tiles + j, 0)),
        ],
        out_specs=[
            pl.BlockSpec((1, tm, D_MODEL), lambda b, j: (b, j, 0)),
            pl.BlockSpec((1, CONV_A_W - 1, W), lambda b, j: (b, 0, 0)),
            pl.BlockSpec((1, CONV_C_W - 1, W), lambda b, j: (b, 0, 0)),
            pl.BlockSpec((1, up_rows, D_FF), lambda b, j: (0, b * tiles + j, 0)),
            pl.BlockSpec((1, down_rows, D_MODEL), lambda b, j: (0, b * tiles + j, 0)),
        ],
        out_shape=[
            jax.ShapeDtypeStruct((bn, seq, D_MODEL), F32),
            jax.ShapeDtypeStruct((bn, CONV_A_W - 1, W), F32),
            jax.ShapeDtypeStruct((bn, CONV_C_W - 1, W), F32),
            jax.ShapeDtypeStruct((1, D_MODEL, D_FF), BF16),
            jax.ShapeDtypeStruct((1, D_FF, D_MODEL), BF16),
        ],
        scratch_shapes=[
            pltpu.VMEM((tm, IN_COLS), F32),
            pltpu.VMEM((tm + HALO_A, W), F32),
            pltpu.VMEM((tm + HALO_C, W), F32),
            pltpu.VMEM((CONV_C_W * SUBLANES, W), F32),
        ],
        compiler_params=pltpu.CompilerParams(
            dimension_semantics=("arbitrary", "arbitrary"), vmem_limit_bytes=VMEM_LIMIT),
        name="prompt_mixer",
    )(x, mk, mv, big["w_in"], p["w_conv_a"], p["ln_v_g"], p["ln_v_b"], p["w_s"], p["b_s_t"],
      p["w_conv_c"], p["b_conv_c"], p["ln_c_g"], p["ln_c_b"], big["w_out_br"], big["w_o"],
      p["ln1_g"], p["ln1_b"], w_up, w_down)


def _sample_attn_kernel(x_ref, wq_ref, k_ref, v_ref, o_ref):
    n_pos, sb, _ = x_ref.shape
    rows = n_pos * sb
    xb = x_ref[...].reshape(rows, D_MODEL).astype(BF16)
    q = _dot(xb, wq_ref[0])
    r_seq = lax.broadcasted_iota(jnp.int32, (HEADS * rows, sb * N_MEM), 0) % sb
    c_seq = lax.broadcasted_iota(jnp.int32, (HEADS * rows, sb * N_MEM), 1) // N_MEM
    own = r_seq == c_seq
    scores = []
    for h in range(HEADS):
        sl = slice(h * HEAD_DIM, (h + 1) * HEAD_DIM)
        kh = k_ref[:, pl.ds(h, N_MEM, stride=HEADS), :].reshape(sb * N_MEM, HEAD_DIM).astype(BF16)
        scores.append(_dot_nt(q[:, sl].astype(BF16), kh))
    sc = jnp.where(own, jnp.concatenate(scores, axis=0) * ATTN_SCALE, -jnp.inf)
    pr = _softmax_rows(sc).astype(BF16)
    heads = []
    for h in range(HEADS):
        vh = v_ref[:, pl.ds(h, N_MEM, stride=HEADS), :].reshape(sb * N_MEM, HEAD_DIM).astype(BF16)
        heads.append(_dot(pr[h * rows:(h + 1) * rows], vh))
    o_ref[...] = jnp.concatenate(heads, axis=1).reshape(n_pos, sb, W)


def _sample_attn(layer, x_t, w_in_bf, cache_k, cache_v):
    n_pos, n_seq, _ = x_t.shape
    sb = SEQ_BLK
    q_block = (7 * W) // W
    return pl.pallas_call(
        _sample_attn_kernel,
        grid=(n_seq // sb,),
        in_specs=[
            pl.BlockSpec((n_pos, sb, D_MODEL), lambda i: (0, i, 0)),
            pl.BlockSpec((1, D_MODEL, W), lambda i: (0, 0, q_block)),
            pl.BlockSpec((None, sb, N_MEM * HEADS, HEAD_DIM), lambda i: (layer, i, 0, 0)),
            pl.BlockSpec((None, sb, N_MEM * HEADS, HEAD_DIM), lambda i: (layer, i, 0, 0)),
        ],
        out_specs=pl.BlockSpec((n_pos, sb, W), lambda i: (0, i, 0)),
        out_shape=jax.ShapeDtypeStruct((n_pos, n_seq, W), F32),
        compiler_params=pltpu.CompilerParams(
            dimension_semantics=("arbitrary",), vmem_limit_bytes=VMEM_LIMIT),
        name="sample_attn",
    )(x_t, w_in_bf, cache_k, cache_v)


def _sample_mixer_kernel(x_ref, yx_ref, sa_ref, sc_ref, w_in_ref, wca_ref, lnvg_ref, lnvb_ref,
                         wsr_ref, bsr_ref, wcc_ref, bcc_ref, lncg_ref, lncb_ref, wbr_ref, wo_ref,
                         ln1g_ref, ln1b_ref,
                         y_ref, p_ref, nsc_ref, v_ref):
    n_pos, n_seq, _ = x_ref.shape
    rows = n_pos * n_seq
    x = x_ref[...].reshape(rows, D_MODEL)
    xb = x.astype(BF16)

    def proj(c0, c1):
        return _dot(xb, w_in_ref[0, :, c0:c1])

    def pos(a, t):
        return a[t * n_seq:(t + 1) * n_seq, :]

    za = proj(0, 3 * W)
    p = za[:, 2 * W:3 * W] * za[:, 0:W]
    p_ref[...] = p.reshape(n_pos, n_seq, W)
    wca = wca_ref[0]
    hist_a = [sa_ref[0, r] for r in range(CONV_A_W - 1)] + [pos(p, t) for t in range(n_pos)]
    conv_a = jnp.concatenate(
        [sum(wca[k:k + 1, :] * hist_a[t + k] for k in range(CONV_A_W)) for t in range(n_pos)], axis=0)
    y_a = za[:, W:2 * W] * conv_a

    zb = _gelu_tanh(proj(3 * W, 5 * W))
    u = zb[:, :W]
    v = _layer_norm(zb[:, W:], lnvg_ref[0], lnvb_ref[0])
    v_ref[...] = v.reshape(n_pos, n_seq, W)
    gate_rows = []
    for t in range(n_pos):
        acc = bsr_ref[0, t:t + 1, :]
        for s_ in range(t + 1):
            acc = acc + wsr_ref[0, t * n_pos + s_:t * n_pos + s_ + 1, :] * pos(v, s_)
        gate_rows.append(acc)
    y_b = u * jnp.concatenate(gate_rows, axis=0)

    zc = proj(5 * W, 7 * W)
    glu = zc[:, :W] * jax.nn.sigmoid(zc[:, W:])
    keep = CONV_C_W - 1 - n_pos
    nsc_ref[0:keep] = sc_ref[0, n_pos:CONV_C_W - 1]
    nsc_ref[keep:CONV_C_W - 1] = glu.reshape(n_pos, n_seq, W)
    wcc = wcc_ref[0]
    conv_rows = []
    for t in range(n_pos):
        acc = None
        for k in range(CONV_C_W):
            r = t + k
            src = sc_ref[0, r] if r < CONV_C_W - 1 else pos(glu, r - (CONV_C_W - 1))
            term = wcc[k:k + 1, :] * src
            acc = term if acc is None else acc + term
        conv_rows.append(acc)
    conv_c = jnp.concatenate(conv_rows, axis=0) + bcc_ref[0]
    y_c = jax.nn.silu(_layer_norm(conv_c, lncg_ref[0], lncb_ref[0]))

    y_x = yx_ref[...].reshape(rows, W)
    gate_pre = lambda n: proj(8 * W + n * D_MODEL, 8 * W + (n + 1) * D_MODEL)
    y = _merge_and_norm(x, gate_pre, (y_a, y_b, y_c, y_x), wbr_ref, wo_ref, ln1g_ref, ln1b_ref)
    y_ref[...] = y.reshape(n_pos, n_seq, D_MODEL)


def _sample_mixer(layer, x_t, yx_t, state_a_t, state_c_t, p, big):
    n_pos, n_seq, _ = x_t.shape
    sb = SEQ_BLK_MIX

    def by_pos(width):
        return pl.BlockSpec((n_pos, sb, width), lambda i: (0, i, 0))

    layer_slice = functools.partial(_resident, layer=layer)

    return pl.pallas_call(
        _sample_mixer_kernel,
        grid=(n_seq // sb,),
        in_specs=[
            by_pos(D_MODEL),
            by_pos(W),
            pl.BlockSpec((1, CONV_A_W - 1, sb, W), lambda i: (layer, 0, i, 0)),
            pl.BlockSpec((1, CONV_C_W - 1, sb, W), lambda i: (layer, 0, i, 0)),
            _resident((D_MODEL, IN_COLS), 0),
            layer_slice((CONV_A_W, W)),
            layer_slice((1, W)),
            layer_slice((1, W)),
            layer_slice((n_pos * n_pos, W)),
            layer_slice((n_pos, W)),
            layer_slice((CONV_C_W, W)),
            layer_slice((1, W)),
            layer_slice((1, W)),
            layer_slice((1, W)),
            _resident((N_BRANCH, W, D_MODEL), 0),
            _resident((D_MODEL, D_MODEL), 0),
            layer_slice((1, D_MODEL)),
            layer_slice((1, D_MODEL)),
        ],
        out_specs=[
            by_pos(D_MODEL),
            by_pos(W),
            pl.BlockSpec((CONV_C_W - 1, sb, W), lambda i: (0, i, 0)),
            by_pos(W),
        ],
        out_shape=[
            jax.ShapeDtypeStruct((n_pos, n_seq, D_MODEL), F32),
            jax.ShapeDtypeStruct((n_pos, n_seq, W), F32),
            jax.ShapeDtypeStruct((CONV_C_W - 1, n_seq, W), F32),
            jax.ShapeDtypeStruct((n_pos, n_seq, W), F32),
        ],
        compiler_params=pltpu.CompilerParams(
            dimension_semantics=("arbitrary",), vmem_limit_bytes=VMEM_LIMIT),
        name="sample_mixer",
    )(x_t, yx_t, state_a_t, state_c_t, big["w_in"], p["w_conv_a"], p["ln_v_g"], p["ln_v_b"],
      p["w_s_rows"], p["b_s_rows"], p["w_conv_c"], p["b_conv_c"], p["ln_c_g"], p["ln_c_b"],
      big["w_out_br"], big["w_o"], p["ln1_g"], p["ln1_b"])


def _mlp_kernel(x_ref, wu_ref, bu_ref, wd_ref, bd_ref, g_ref, b_ref, *rest):
    n_side = (len(rest) - 1) // 2
    y_ref = rest[n_side]
    for src, dst in zip(rest[:n_side], rest[n_side + 1:]):
        dst[...] = src[...].astype(BF16)
    x = x_ref[...]
    h = jnp.maximum(_dot(x.astype(BF16), wu_ref[0]) + bu_ref[0], 0.0)
    h = _dot((h * h).astype(BF16), wd_ref[0]) + bd_ref[0]
    y_ref[...] = _layer_norm(ALPHA * x + h, g_ref[0], b_ref[0])


def _mlp(layer, x2d, p, wu16, wd16, round_next=()):
    rows = x2d.shape[0]
    tm = min(TM_MLP, rows)
    n_steps = rows // tm
    side_in, side_out, side_shape = [], [], []
    for w32, l_next in round_next:
        _, r, c = w32.shape
        slab = r // n_steps
        assert slab * n_steps == r and slab % BF16_SUBLANES == 0
        side_in.append(pl.BlockSpec((1, slab, c), functools.partial(lambda i, l: (l, i, 0), l=l_next)))
        side_out.append(pl.BlockSpec((1, slab, c), lambda i: (0, i, 0)))
        side_shape.append(jax.ShapeDtypeStruct((1, r, c), BF16))
    out = pl.pallas_call(
        _mlp_kernel,
        grid=(n_steps,),
        in_specs=[
            pl.BlockSpec((tm, D_MODEL), lambda i: (i, 0)),
            _resident((D_MODEL, D_FF), 0),
            _resident((1, D_FF), layer),
            _resident((D_FF, D_MODEL), 0),
            _resident((1, D_MODEL), layer),
            _resident((1, D_MODEL), layer),
            _resident((1, D_MODEL), layer),
        ] + side_in,
        out_specs=[pl.BlockSpec((tm, D_MODEL), lambda i: (i, 0))] + side_out,
        out_shape=[jax.ShapeDtypeStruct((rows, D_MODEL), F32)] + side_shape,
        compiler_params=pltpu.CompilerParams(
            dimension_semantics=("arbitrary",), vmem_limit_bytes=VMEM_LIMIT),
        name="mlp",
    )(x2d, wu16, p["b_up"], wd16, p["b_down"], p["ln2_g"], p["ln2_b"], *[w for w, _ in round_next])
    return out[0], out[1:]


def kernel(x_prompt, x_sample, mem_prompt, state_conv_a, state_conv_c, cache_mem_k, cache_mem_v,
           w_in, w_conv_a, ln_v_g, ln_v_b, w_s, b_s, w_conv_c, b_conv_c, ln_c_g, ln_c_b,
           w_mem_kv, w_out_br, w_o, ln1_g, ln1_b, w_up, b_up, w_down, b_down, ln2_g, ln2_b):
    bp, seq, _ = x_prompt.shape
    n_seq, n_pos, _ = x_sample.shape
    assert seq % TM_MIX == 0 and TM_MIX % CHUNK == 0 and TM_MIX % CONV_RB == 0
    assert n_seq % SEQ_BLK == 0 and n_seq % SEQ_BLK_MIX == 0 and n_pos <= CHUNK and n_pos >= CONV_A_W - 1

    row = lambda a: a[:, None, :]
    params = {
        "w_conv_a": w_conv_a, "ln_v_g": row(ln_v_g), "ln_v_b": row(ln_v_b),
        "w_s": w_s, "b_s_t": jnp.swapaxes(b_s, 1, 2),
        "w_s_rows": jnp.repeat(
            jnp.transpose(w_s[:, :, :n_pos, :n_pos], (0, 2, 3, 1)).reshape(DEPTH, n_pos * n_pos, GROUPS),
            GROUP_W, axis=-1),
        "b_s_rows": jnp.repeat(jnp.swapaxes(b_s[:, :, :n_pos], 1, 2), GROUP_W, axis=-1),
        "w_conv_c": w_conv_c, "b_conv_c": row(b_conv_c), "ln_c_g": row(ln_c_g), "ln_c_b": row(ln_c_b),
        "ln1_g": row(ln1_g), "ln1_b": row(ln1_b),
        "b_up": row(b_up), "b_down": row(b_down), "ln2_g": row(ln2_g), "ln2_b": row(ln2_b),
    }
    w_br2d = w_out_br.reshape(DEPTH, N_BRANCH * W, D_MODEL)
    big = {"w_in": w_in[0:1].astype(BF16), "w_out_br": w_out_br[0:1].astype(BF16),
           "w_o": w_o[0:1].astype(BF16)}

    pk, pv = _kv_project(mem_prompt.reshape(bp * N_MEM, D_MODEL), w_mem_kv)
    pk = pk.reshape(DEPTH, bp, N_MEM * HEADS, HEAD_DIM)
    pv = pv.reshape(DEPTH, bp, N_MEM * HEADS, HEAD_DIM)

    ys = jnp.swapaxes(x_sample, 0, 1)
    state_a_t = jnp.swapaxes(state_conv_a, 1, 2)
    state_c_t = jnp.swapaxes(state_conv_c, 1, 2)
    cache_k = cache_mem_k.reshape(DEPTH, n_seq, N_MEM * HEADS, HEAD_DIM)
    cache_v = cache_mem_v.reshape(DEPTH, n_seq, N_MEM * HEADS, HEAD_DIM)

    yp = x_prompt
    pa, pc, sa, sc, sv = [], [], [], [], []
    for l in range(DEPTH):
        x1, nba, nbc, wu16, wd16 = _prompt_mixer(l, yp, pk, pv, params, big, w_up, w_down)
        nxt = [(w_in, l + 1), (w_br2d, l + 1), (w_o, l + 1)] if l + 1 < DEPTH else []
        yp, rounded = _mlp(l, x1.reshape(bp * seq, D_MODEL), params, wu16, wd16, nxt)
        yp = yp.reshape(bp, seq, D_MODEL)
        pa.append(nba)
        pc.append(nbc)

        yx = _sample_attn(l, ys, big["w_in"], cache_k, cache_v)
        s1, p_t, nsc_t, v_t = _sample_mixer(l, ys, yx, state_a_t, state_c_t, params, big)
        ys, _ = _mlp(l, s1.reshape(n_pos * n_seq, D_MODEL), params, wu16, wd16)
        ys = ys.reshape(n_pos, n_seq, D_MODEL)
        sa.append(jnp.swapaxes(p_t[n_pos - (CONV_A_W - 1):], 0, 1))
        sc.append(nsc_t)
        sv.append(jnp.swapaxes(v_t, 0, 1))
        if rounded:
            big = {"w_in": rounded[0], "w_out_br": rounded[1].reshape(1, N_BRANCH, W, D_MODEL),
                   "w_o": rounded[2]}

    return (yp, jnp.swapaxes(ys, 0, 1), jnp.stack(pa), jnp.stack(pc),
            pk.reshape(DEPTH, bp, N_MEM, HEADS, HEAD_DIM), pv.reshape(DEPTH, bp, N_MEM, HEADS, HEAD_DIM),
            jnp.stack(sa), jnp.swapaxes(jnp.stack(sc), 1, 2), jnp.stack(sv))
```

```python
import functools
import math

import jax
import jax.numpy as jnp
from jax import lax
from jax.experimental import pallas as pl
from jax.experimental.pallas import tpu as pltpu

D_MODEL = 1024
DEPTH = 4
N_MEM = 256
W = D_MODEL // 2
CONV_A_W = 3
CHUNK = 128
GROUPS = 4
GROUP_W = W // GROUPS
CONV_C_W = 31
HEADS = 4
HEAD_DIM = W // HEADS
D_FF = 4 * D_MODEL
N_BRANCH = 4
IN_COLS = 8 * W + N_BRANCH * D_MODEL
ALPHA = (2 * DEPTH) ** 0.25
LN_EPS = 1e-5
ATTN_SCALE = HEAD_DIM ** -0.5

V7X_VMEM_BYTES = 64 * 1024 * 1024
VMEM_LIMIT = V7X_VMEM_BYTES - 8 * 1024 * 1024
SUBLANES = 8
BF16_SUBLANES = 16
LANES = 128

TM_MIX = 256
TM_MLP = 512
HALO_A = 8
HALO_C = 32
CONV_RB = 64
SEQ_BLK = 16
ATTN_GROUP = 8
SEQ_BLK_MIX = 32

F32 = jnp.float32
BF16 = jnp.bfloat16


def _layer_norm(x, g, b):
    mu = jnp.mean(x, axis=-1, keepdims=True)
    xc = x - mu
    var = jnp.mean(xc * xc, axis=-1, keepdims=True)
    return xc * lax.rsqrt(var + LN_EPS) * g + b


def _dot(a, b):
    return jnp.dot(a, b, preferred_element_type=F32)


def _dot_nt(a, b):
    return lax.dot_general(a, b, (((1,), (1,)), ((), ())), preferred_element_type=F32)


def _gelu_tanh(x):
    c1 = math.sqrt(2.0 / math.pi)
    u = x * (c1 + (c1 * 0.044715) * (x * x))
    h = 0.5 * x
    return h + h * jnp.tanh(u)


def _softmax_rows(s):
    m = jnp.max(s, axis=-1, keepdims=True)
    e = jnp.exp(s - m)
    return e / jnp.sum(e, axis=-1, keepdims=True)


def _merge_and_norm(x, gate_pre, ys, wbr_ref, wo_ref, ln1g_ref, ln1b_ref):
    acc = None
    for n in range(N_BRANCH):
        term = jax.nn.sigmoid(gate_pre(n)) * _dot(ys[n].astype(BF16), wbr_ref[0, n])
        acc = term if acc is None else acc + term
    mix = _dot(acc.astype(BF16), wo_ref[0])
    return _layer_norm(ALPHA * x + mix, ln1g_ref[0], ln1b_ref[0])


def _kv_kernel(mem_ref, w_ref, k_ref, v_ref):
    rows = mem_ref.shape[0]
    kv = _dot(mem_ref[...].astype(BF16), w_ref[0].astype(BF16))
    for h in range(HEADS):
        k_ref[0, pl.ds(h, rows, stride=HEADS), :] = kv[:, h * HEAD_DIM:(h + 1) * HEAD_DIM]
        v_ref[0, pl.ds(h, rows, stride=HEADS), :] = kv[:, W + h * HEAD_DIM:W + (h + 1) * HEAD_DIM]


def _kv_project(mem2d, w_kv):
    rows = mem2d.shape[0]
    return pl.pallas_call(
        _kv_kernel,
        grid=(DEPTH,),
        in_specs=[
            pl.BlockSpec((rows, D_MODEL), lambda l: (0, 0)),
            pl.BlockSpec((1, D_MODEL, 2 * W), lambda l: (l, 0, 0)),
        ],
        out_specs=[
            pl.BlockSpec((1, rows * HEADS, HEAD_DIM), lambda l: (l, 0, 0)),
            pl.BlockSpec((1, rows * HEADS, HEAD_DIM), lambda l: (l, 0, 0)),
        ],
        out_shape=[jax.ShapeDtypeStruct((DEPTH, rows * HEADS, HEAD_DIM), F32)] * 2,
        compiler_params=pltpu.CompilerParams(
            dimension_semantics=("arbitrary",), vmem_limit_bytes=VMEM_LIMIT),
        name="kv_project",
    )(mem2d, w_kv)


def _conv_c_rows(ext_c, wb_ref, r0):
    lead = HALO_C - (CONV_C_W - 1)
    span = CONV_RB + HALO_C
    cols = []
    for c in range(W // LANES):
        lanes = slice(c * LANES, (c + 1) * LANES)
        win = ext_c[r0:r0 + span, lanes]
        acc = None
        for b in range(SUBLANES):
            sh = win if b == 0 else pltpu.roll(win, span - b, axis=0)
            for a in range(HALO_C // SUBLANES + 1):
                k = a * SUBLANES + b - lead
                if 0 <= k < CONV_C_W:
                    tap = sh[a * SUBLANES:a * SUBLANES + CONV_RB].reshape(CONV_RB // SUBLANES, SUBLANES, LANES)
                    term = wb_ref[k * SUBLANES:(k + 1) * SUBLANES, lanes][None] * tap
                    acc = term if acc is None else acc + term
        cols.append(acc.reshape(CONV_RB, LANES))
    return jnp.concatenate(cols, axis=1)


def _mixer_stage(z_ref, x_ref, mk_ref, mv_ref, wca_ref, lnvg_ref, lnvb_ref, ws_ref, bst_ref,
                 bcc_ref, lncg_ref, lncb_ref, wbr_ref, wo_ref, ln1g_ref, ln1b_ref,
                 y_ref, nba_ref, nbc_ref, ext_a, ext_c, wb_c):
    tm = z_ref.shape[0]
    n_chunks = tm // CHUNK

    glu = z_ref[:, 5 * W:6 * W] * jax.nn.sigmoid(z_ref[:, 6 * W:7 * W])
    ext_c[HALO_C:HALO_C + tm, :] = glu
    conv_c = jnp.concatenate(
        [_conv_c_rows(ext_c, wb_c, rb * CONV_RB) for rb in range(tm // CONV_RB)], axis=0) + bcc_ref[0]
    nbc_ref[0] = ext_c[HALO_C + tm - (CONV_C_W - 1):HALO_C + tm, :]
    ext_c[0:HALO_C, :] = ext_c[tm:tm + HALO_C, :]
    y_c = jax.nn.silu(_layer_norm(conv_c, lncg_ref[0], lncb_ref[0]))

    p = z_ref[:, 2 * W:3 * W] * z_ref[:, 0:W]
    ext_a[HALO_A:HALO_A + tm, :] = p
    wca = wca_ref[0]
    conv_a = (wca[0:1, :] * ext_a[HALO_A - 2:HALO_A - 2 + tm, :]
              + wca[1:2, :] * ext_a[HALO_A - 1:HALO_A - 1 + tm, :]
              + wca[2:3, :] * p)
    y_a = z_ref[:, W:2 * W] * conv_a
    nba_ref[0] = ext_a[HALO_A + tm - 2:HALO_A + tm, :]
    ext_a[0:HALO_A, :] = ext_a[tm:tm + HALO_A, :]

    u = _gelu_tanh(z_ref[:, 3 * W:4 * W])
    v = _layer_norm(_gelu_tanh(z_ref[:, 4 * W:5 * W]), lnvg_ref[0], lnvb_ref[0]).astype(BF16)
    row = lax.broadcasted_iota(jnp.int32, (CHUNK, CHUNK), 0)
    col = lax.broadcasted_iota(jnp.int32, (CHUNK, CHUNK), 1)
    causal = row >= col
    gate_cols = []
    for g in range(GROUPS):
        wg = jnp.where(causal, ws_ref[0, g], 0.0).astype(BF16)
        rhs = jnp.concatenate(
            [v[n * CHUNK:(n + 1) * CHUNK, g * GROUP_W:(g + 1) * GROUP_W] for n in range(n_chunks)],
            axis=1)
        gate_cols.append(_dot(wg, rhs) + bst_ref[0, :, g:g + 1])
    s = jnp.concatenate(
        [jnp.concatenate([gate_cols[g][:, n * GROUP_W:(n + 1) * GROUP_W] for g in range(GROUPS)], axis=1)
         for n in range(n_chunks)], axis=0)
    y_b = u * s

    scores = []
    for h in range(HEADS):
        qh = z_ref[:, 7 * W + h * HEAD_DIM:7 * W + (h + 1) * HEAD_DIM].astype(BF16)
        kh = mk_ref[0, 0, pl.ds(h, N_MEM, stride=HEADS), :].astype(BF16)
        scores.append(_dot_nt(qh, kh))
    pr = _softmax_rows(jnp.concatenate(scores, axis=0) * ATTN_SCALE).astype(BF16)
    heads = []
    for h in range(HEADS):
        vh = mv_ref[0, 0, pl.ds(h, N_MEM, stride=HEADS), :].astype(BF16)
        heads.append(_dot(pr[h * tm:(h + 1) * tm], vh))
    y_x = jnp.concatenate(heads, axis=1)

    gate_pre = lambda n: z_ref[:, 8 * W + n * D_MODEL:8 * W + (n + 1) * D_MODEL]
    y_ref[0] = _merge_and_norm(x_ref[0], gate_pre, (y_a, y_b, y_c, y_x), wbr_ref, wo_ref,
                               ln1g_ref, ln1b_ref)


def _prompt_mixer_kernel(x_ref, mk_ref, mv_ref, w_in_ref, wca_ref, lnvg_ref, lnvb_ref,
                         ws_ref, bst_ref, wcc_ref, bcc_ref, lncg_ref, lncb_ref, wbr_ref, wo_ref,
                         ln1g_ref, ln1b_ref, wu32_ref, wd32_ref,
                         y_ref, nba_ref, nbc_ref, wu16_ref, wd16_ref, z_ref, ext_a, ext_c, wb_c):
    wu16_ref[...] = wu32_ref[...].astype(BF16)
    wd16_ref[...] = wd32_ref[...].astype(BF16)

    @pl.when(pl.program_id(1) == 0)
    def _():
        ext_a[0:HALO_A, :] = jnp.zeros((HALO_A, W), F32)
        ext_c[0:HALO_C, :] = jnp.zeros((HALO_C, W), F32)
        for k in range(CONV_C_W):
            wb_c[k * SUBLANES:(k + 1) * SUBLANES, :] = jnp.broadcast_to(wcc_ref[0, k:k + 1, :], (SUBLANES, W))

    x = x_ref[0]
    xb = x.astype(BF16)
    for c0, c1 in ((5 * W, 7 * W), (0, 3 * W), (3 * W, 5 * W), (7 * W, 8 * W)):
        z_ref[:, c0:c1] = _dot(xb, w_in_ref[0, :, c0:c1])
    for c0 in range(8 * W, IN_COLS, D_MODEL):
        z_ref[:, c0:c0 + D_MODEL] = _dot(xb, w_in_ref[0, :, c0:c0 + D_MODEL])
    _mixer_stage(z_ref, x_ref, mk_ref, mv_ref, wca_ref, lnvg_ref, lnvb_ref, ws_ref,
                 bst_ref, bcc_ref, lncg_ref, lncb_ref, wbr_ref, wo_ref, ln1g_ref, ln1b_ref,
                 y_ref, nba_ref, nbc_ref, ext_a, ext_c, wb_c)


def _resident(shape_tail, layer):
    nd = len(shape_tail)
    return pl.BlockSpec((1,) + tuple(shape_tail), lambda *_: (layer,) + (0,) * nd,
                        pipeline_mode=pl.Buffered(1))


def _prompt_mixer(layer, x, mk, mv, p, big, w_up, w_down):
    bn, seq, _ = x.shape
    tm = TM_MIX
    tiles = seq // tm
    n_steps = bn * tiles
    up_rows, down_rows = D_MODEL // n_steps, D_FF // n_steps
    assert up_rows * n_steps == D_MODEL and up_rows % BF16_SUBLANES == 0
    return pl.pallas_call(
        _prompt_mixer_kernel,
        grid=(bn, tiles),
        in_specs=[
            pl.BlockSpec((1, tm, D_MODEL), lambda b, j: (b, j, 0)),
            pl.BlockSpec((1, 1, N_MEM * HEADS, HEAD_DIM), lambda b, j: (layer, b, 0, 0)),
            pl.BlockSpec((1, 1, N_MEM * HEADS, HEAD_DIM), lambda b, j: (layer, b, 0, 0)),
            _resident((D_MODEL, IN_COLS), 0),
            _resident((CONV_A_W, W), layer),
            _resident((1, W), layer),
            _resident((1, W), layer),
            _resident((GROUPS, CHUNK, CHUNK), layer),
            _resident((CHUNK, GROUPS), layer),
            _resident((CONV_C_W, W), layer),
            _resident((1, W), layer),
            _resident((1, W), layer),
            _resident((1, W), layer),
            _resident((N_BRANCH, W, D_MODEL), 0),
            _resident((D_MODEL, D_MODEL), 0),
            _resident((1, D_MODEL), layer),
            _resident((1, D_MODEL), layer),
            pl.BlockSpec((1, up_rows, D_FF), lambda b, j: (layer, b * tiles + j, 0)),
            pl.BlockSpec((1, down_rows, D_MODEL), lambda b, j: (layer, b * tiles + j, 0)),
        ],
        out_specs=[
            pl.BlockSpec((1, tm, D_MODEL), lambda b, j: (b, j, 0)),
            pl.BlockSpec((1, CONV_A_W - 1, W), lambda b, j: (b, 0, 0)),
            pl.BlockSpec((1, CONV_C_W - 1, W), lambda b, j: (b, 0, 0)),
            pl.BlockSpec((1, up_rows, D_FF), lambda b, j: (0, b * tiles + j, 0)),
            pl.BlockSpec((1, down_rows, D_MODEL), lambda b, j: (0, b * tiles + j, 0)),
        ],
        out_shape=[
            jax.ShapeDtypeStruct((bn, seq, D_MODEL), F32),
            jax.ShapeDtypeStruct((bn, CONV_A_W - 1, W), F32),
            jax.ShapeDtypeStruct((bn, CONV_C_W - 1, W), F32),
            jax.ShapeDtypeStruct((1, D_MODEL, D_FF), BF16),
            jax.ShapeDtypeStruct((1, D_FF, D_MODEL), BF16),
        ],
        scratch_shapes=[
            pltpu.VMEM((tm, IN_COLS), F32),
            pltpu.VMEM((tm + HALO_A, W), F32),
            pltpu.VMEM((tm + HALO_C, W), F32),
            pltpu.VMEM((CONV_C_W * SUBLANES, W), F32),
        ],
        compiler_params=pltpu.CompilerParams(
            dimension_semantics=("arbitrary", "arbitrary"), vmem_limit_bytes=VMEM_LIMIT),
        name="prompt_mixer",
    )(x, mk, mv, big["w_in"], p["w_conv_a"], p["ln_v_g"], p["ln_v_b"], p["w_s"], p["b_s_t"],
      p["w_conv_c"], p["b_conv_c"], p["ln_c_g"], p["ln_c_b"], big["w_out_br"], big["w_o"],
      p["ln1_g"], p["ln1_b"], w_up, w_down)


def _sample_attn_kernel(x_ref, wq_ref, k_ref, v_ref, o_ref):
    n_pos = x_ref.shape[0]
    sb = ATTN_GROUP
    rows = n_pos * sb
    r_seq = lax.broadcasted_iota(jnp.int32, (HEADS * rows, sb * N_MEM), 0) % sb
    c_seq = lax.broadcasted_iota(jnp.int32, (HEADS * rows, sb * N_MEM), 1) // N_MEM
    own = r_seq == c_seq
    for g0 in range(0, x_ref.shape[1], sb):
        xb = x_ref[:, g0:g0 + sb, :].reshape(rows, D_MODEL).astype(BF16)
        q = _dot(xb, wq_ref[0])
        scores = []
        for h in range(HEADS):
            sl = slice(h * HEAD_DIM, (h + 1) * HEAD_DIM)
            kh = k_ref[g0:g0 + sb, pl.ds(h, N_MEM, stride=HEADS), :].reshape(sb * N_MEM, HEAD_DIM)
            scores.append(_dot_nt(q[:, sl].astype(BF16), kh.astype(BF16)))
        sc = jnp.where(own, jnp.concatenate(scores, axis=0) * ATTN_SCALE, -jnp.inf)
        pr = _softmax_rows(sc).astype(BF16)
        heads = []
        for h in range(HEADS):
            vh = v_ref[g0:g0 + sb, pl.ds(h, N_MEM, stride=HEADS), :].reshape(sb * N_MEM, HEAD_DIM)
            heads.append(_dot(pr[h * rows:(h + 1) * rows], vh.astype(BF16)))
        o_ref[:, g0:g0 + sb, :] = jnp.concatenate(heads, axis=1).reshape(n_pos, sb, W)


def _sample_attn(layer, x_t, w_in_bf, cache_k, cache_v):
    n_pos, n_seq, _ = x_t.shape
    sb = SEQ_BLK
    q_block = (7 * W) // W
    return pl.pallas_call(
        _sample_attn_kernel,
        grid=(n_seq // sb,),
        in_specs=[
            pl.BlockSpec((n_pos, sb, D_MODEL), lambda i: (0, i, 0)),
            pl.BlockSpec((1, D_MODEL, W), lambda i: (0, 0, q_block)),
            pl.BlockSpec((None, sb, N_MEM * HEADS, HEAD_DIM), lambda i: (layer, i, 0, 0)),
            pl.BlockSpec((None, sb, N_MEM * HEADS, HEAD_DIM), lambda i: (layer, i, 0, 0)),
        ],
        out_specs=pl.BlockSpec((n_pos, sb, W), lambda i: (0, i, 0)),
        out_shape=jax.ShapeDtypeStruct((n_pos, n_seq, W), F32),
        compiler_params=pltpu.CompilerParams(
            dimension_semantics=("arbitrary",), vmem_limit_bytes=VMEM_LIMIT),
        name="sample_attn",
    )(x_t, w_in_bf, cache_k, cache_v)


def _sample_mixer_kernel(x_ref, yx_ref, sa_ref, sc_ref, w_in_ref, wca_ref, lnvg_ref, lnvb_ref,
                         wsr_ref, bsr_ref, wcc_ref, bcc_ref, lncg_ref, lncb_ref, wbr_ref, wo_ref,
                         ln1g_ref, ln1b_ref,
                         y_ref, p_ref, nsc_ref, v_ref):
    n_pos, n_seq, _ = x_ref.shape
    rows = n_pos * n_seq
    x = x_ref[...].reshape(rows, D_MODEL)
    xb = x.astype(BF16)

    def proj(c0, c1):
        return _dot(xb, w_in_ref[0, :, c0:c1])

    def pos(a, t):
        return a[t * n_seq:(t + 1) * n_seq, :]

    za = proj(0, 3 * W)
    p = za[:, 2 * W:3 * W] * za[:, 0:W]
    p_ref[...] = p.reshape(n_pos, n_seq, W)
    wca = wca_ref[0]
    hist_a = [sa_ref[0, r] for r in range(CONV_A_W - 1)] + [pos(p, t) for t in range(n_pos)]
    conv_a = jnp.concatenate(
        [sum(wca[k:k + 1, :] * hist_a[t + k] for k in range(CONV_A_W)) for t in range(n_pos)], axis=0)
    y_a = za[:, W:2 * W] * conv_a

    zb = _gelu_tanh(proj(3 * W, 5 * W))
    u = zb[:, :W]
    v = _layer_norm(zb[:, W:], lnvg_ref[0], lnvb_ref[0])
    v_ref[...] = v.reshape(n_pos, n_seq, W)
    gate_rows = []
    for t in range(n_pos):
        acc = bsr_ref[0, t:t + 1, :]
        for s_ in range(t + 1):
            acc = acc + wsr_ref[0, t * n_pos + s_:t * n_pos + s_ + 1, :] * pos(v, s_)
        gate_rows.append(acc)
    y_b = u * jnp.concatenate(gate_rows, axis=0)

    zc = proj(5 * W, 7 * W)
    glu = zc[:, :W] * jax.nn.sigmoid(zc[:, W:])
    keep = CONV_C_W - 1 - n_pos
    nsc_ref[0:keep] = sc_ref[0, n_pos:CONV_C_W - 1]
    nsc_ref[keep:CONV_C_W - 1] = glu.reshape(n_pos, n_seq, W)
    wcc = wcc_ref[0]
    conv_rows = []
    for t in range(n_pos):
        acc = None
        for k in range(CONV_C_W):
            r = t + k
            src = sc_ref[0, r] if r < CONV_C_W - 1 else pos(glu, r - (CONV_C_W - 1))
            term = wcc[k:k + 1, :] * src
            acc = term if acc is None else acc + term
        conv_rows.append(acc)
    conv_c = jnp.concatenate(conv_rows, axis=0) + bcc_ref[0]
    y_c = jax.nn.silu(_layer_norm(conv_c, lncg_ref[0], lncb_ref[0]))

    y_x = yx_ref[...].reshape(rows, W)
    gate_pre = lambda n: proj(8 * W + n * D_MODEL, 8 * W + (n + 1) * D_MODEL)
    y = _merge_and_norm(x, gate_pre, (y_a, y_b, y_c, y_x), wbr_ref, wo_ref, ln1g_ref, ln1b_ref)
    y_ref[...] = y.reshape(n_pos, n_seq, D_MODEL)


def _sample_mixer(layer, x_t, yx_t, state_a_t, state_c_t, p, big):
    n_pos, n_seq, _ = x_t.shape
    sb = SEQ_BLK_MIX

    def by_pos(width):
        return pl.BlockSpec((n_pos, sb, width), lambda i: (0, i, 0))

    layer_slice = functools.partial(_resident, layer=layer)

    return pl.pallas_call(
        _sample_mixer_kernel,
        grid=(n_seq // sb,),
        in_specs=[
            by_pos(D_MODEL),
            by_pos(W),
            pl.BlockSpec((1, CONV_A_W - 1, sb, W), lambda i: (layer, 0, i, 0)),
            pl.BlockSpec((1, CONV_C_W - 1, sb, W), lambda i: (layer, 0, i, 0)),
            _resident((D_MODEL, IN_COLS), 0),
            layer_slice((CONV_A_W, W)),
            layer_slice((1, W)),
            layer_slice((1, W)),
            layer_slice((n_pos * n_pos, W)),
            layer_slice((n_pos, W)),
            layer_slice((CONV_C_W, W)),
            layer_slice((1, W)),
            layer_slice((1, W)),
            layer_slice((1, W)),
            _resident((N_BRANCH, W, D_MODEL), 0),
            _resident((D_MODEL, D_MODEL), 0),
            layer_slice((1, D_MODEL)),
            layer_slice((1, D_MODEL)),
        ],
        out_specs=[
            by_pos(D_MODEL),
            by_pos(W),
            pl.BlockSpec((CONV_C_W - 1, sb, W), lambda i: (0, i, 0)),
            by_pos(W),
        ],
        out_shape=[
            jax.ShapeDtypeStruct((n_pos, n_seq, D_MODEL), F32),
            jax.ShapeDtypeStruct((n_pos, n_seq, W), F32),
            jax.ShapeDtypeStruct((CONV_C_W - 1, n_seq, W), F32),
            jax.ShapeDtypeStruct((n_pos, n_seq, W), F32),
        ],
        compiler_params=pltpu.CompilerParams(
            dimension_semantics=("arbitrary",), vmem_limit_bytes=VMEM_LIMIT),
        name="sample_mixer",
    )(x_t, yx_t, state_a_t, state_c_t, big["w_in"], p["w_conv_a"], p["ln_v_g"], p["ln_v_b"],
      p["w_s_rows"], p["b_s_rows"], p["w_conv_c"], p["b_conv_c"], p["ln_c_g"], p["ln_c_b"],
      big["w_out_br"], big["w_o"], p["ln1_g"], p["ln1_b"])


def _mlp_kernel(x_ref, wu_ref, bu_ref, wd_ref, bd_ref, g_ref, b_ref, *rest):
    n_side = (len(rest) - 1) // 2
    y_ref = rest[n_side]
    for src, dst in zip(rest[:n_side], rest[n_side + 1:]):
        dst[...] = src[...].astype(BF16)
    x = x_ref[...]
    h = jnp.maximum(_dot(x.astype(BF16), wu_ref[0]) + bu_ref[0], 0.0)
    h = _dot((h * h).astype(BF16), wd_ref[0]) + bd_ref[0]
    y_ref[...] = _layer_norm(ALPHA * x + h, g_ref[0], b_ref[0])


def _mlp(layer, x2d, p, wu16, wd16, round_next=()):
    rows = x2d.shape[0]
    tm = min(TM_MLP, rows)
    n_steps = rows // tm
    side_in, side_out, side_shape = [], [], []
    for w32, l_next in round_next:
        _, r, c = w32.shape
        slab = r // n_steps
        assert slab * n_steps == r and slab % BF16_SUBLANES == 0
        side_in.append(pl.BlockSpec((1, slab, c), functools.partial(lambda i, l: (l, i, 0), l=l_next)))
        side_out.append(pl.BlockSpec((1, slab, c), lambda i: (0, i, 0)))
        side_shape.append(jax.ShapeDtypeStruct((1, r, c), BF16))
    out = pl.pallas_call(
        _mlp_kernel,
        grid=(n_steps,),
        in_specs=[
            pl.BlockSpec((tm, D_MODEL), lambda i: (i, 0)),
            _resident((D_MODEL, D_FF), 0),
            _resident((1, D_FF), layer),
            _resident((D_FF, D_MODEL), 0),
            _resident((1, D_MODEL), layer),
            _resident((1, D_MODEL), layer),
            _resident((1, D_MODEL), layer),
        ] + side_in,
        out_specs=[pl.BlockSpec((tm, D_MODEL), lambda i: (i, 0))] + side_out,
        out_shape=[jax.ShapeDtypeStruct((rows, D_MODEL), F32)] + side_shape,
        compiler_params=pltpu.CompilerParams(
            dimension_semantics=("arbitrary",), vmem_limit_bytes=VMEM_LIMIT),
        name="mlp",
    )(x2d, wu16, p["b_up"], wd16, p["b_down"], p["ln2_g"], p["ln2_b"], *[w for w, _ in round_next])
    return out[0], out[1:]


def kernel(x_prompt, x_sample, mem_prompt, state_conv_a, state_conv_c, cache_mem_k, cache_mem_v,
           w_in, w_conv_a, ln_v_g, ln_v_b, w_s, b_s, w_conv_c, b_conv_c, ln_c_g, ln_c_b,
           w_mem_kv, w_out_br, w_o, ln1_g, ln1_b, w_up, b_up, w_down, b_down, ln2_g, ln2_b):
    bp, seq, _ = x_prompt.shape
    n_seq, n_pos, _ = x_sample.shape
    assert seq % TM_MIX == 0 and TM_MIX % CHUNK == 0 and TM_MIX % CONV_RB == 0
    assert n_seq % SEQ_BLK == 0 and SEQ_BLK % ATTN_GROUP == 0 and n_seq % SEQ_BLK_MIX == 0
    assert CONV_A_W - 1 <= n_pos <= CHUNK

    row = lambda a: a[:, None, :]
    params = {
        "w_conv_a": w_conv_a, "ln_v_g": row(ln_v_g), "ln_v_b": row(ln_v_b),
        "w_s": w_s, "b_s_t": jnp.swapaxes(b_s, 1, 2),
        "w_s_rows": jnp.repeat(
            jnp.transpose(w_s[:, :, :n_pos, :n_pos], (0, 2, 3, 1)).reshape(DEPTH, n_pos * n_pos, GROUPS),
            GROUP_W, axis=-1),
        "b_s_rows": jnp.repeat(jnp.swapaxes(b_s[:, :, :n_pos], 1, 2), GROUP_W, axis=-1),
        "w_conv_c": w_conv_c, "b_conv_c": row(b_conv_c), "ln_c_g": row(ln_c_g), "ln_c_b": row(ln_c_b),
        "ln1_g": row(ln1_g), "ln1_b": row(ln1_b),
        "b_up": row(b_up), "b_down": row(b_down), "ln2_g": row(ln2_g), "ln2_b": row(ln2_b),
    }
    w_br2d = w_out_br.reshape(DEPTH, N_BRANCH * W, D_MODEL)
    big = {"w_in": w_in[0:1].astype(BF16), "w_out_br": w_out_br[0:1].astype(BF16),
           "w_o": w_o[0:1].astype(BF16)}

    pk, pv = _kv_project(mem_prompt.reshape(bp * N_MEM, D_MODEL), w_mem_kv)
    pk = pk.reshape(DEPTH, bp, N_MEM * HEADS, HEAD_DIM)
    pv = pv.reshape(DEPTH, bp, N_MEM * HEADS, HEAD_DIM)

    ys = jnp.swapaxes(x_sample, 0, 1)
    state_a_t = jnp.swapaxes(state_conv_a, 1, 2)
    state_c_t = jnp.swapaxes(state_conv_c, 1, 2)
    cache_k = cache_mem_k.reshape(DEPTH, n_seq, N_MEM * HEADS, HEAD_DIM)
    cache_v = cache_mem_v.reshape(DEPTH, n_seq, N_MEM * HEADS, HEAD_DIM)

    yp = x_prompt
    pa, pc, sa, sc, sv = [], [], [], [], []
    for l in range(DEPTH):
        x1, nba, nbc, wu16, wd16 = _prompt_mixer(l, yp, pk, pv, params, big, w_up, w_down)
        nxt = [(w_in, l + 1), (w_br2d, l + 1), (w_o, l + 1)] if l + 1 < DEPTH else []
        yp, rounded = _mlp(l, x1.reshape(bp * seq, D_MODEL), params, wu16, wd16, nxt)
        yp = yp.reshape(bp, seq, D_MODEL)
        pa.append(nba)
        pc.append(nbc)

        yx = _sample_attn(l, ys, big["w_in"], cache_k, cache_v)
        s1, p_t, nsc_t, v_t = _sample_mixer(l, ys, yx, state_a_t, state_c_t, params, big)
        ys, _ = _mlp(l, s1.reshape(n_pos * n_seq, D_MODEL), params, wu16, wd16)
        ys = ys.reshape(n_pos, n_seq, D_MODEL)
        sa.append(jnp.swapaxes(p_t[n_pos - (CONV_A_W - 1):], 0, 1))
        sc.append(nsc_t)
        sv.append(jnp.swapaxes(v_t, 0, 1))
        if rounded:
            big = {"w_in": rounded[0], "w_out_br": rounded[1].reshape(1, N_BRANCH, W, D_MODEL),
                   "w_o": rounded[2]}

    return (yp, jnp.swapaxes(ys, 0, 1), jnp.stack(pa), jnp.stack(pc),
            pk.reshape(DEPTH, bp, N_MEM, HEADS, HEAD_DIM), pv.reshape(DEPTH, bp, N_MEM, HEADS, HEAD_DIM),
            jnp.stack(sa), jnp.swapaxes(jnp.stack(sc), 1, 2), jnp.stack(sv))
```

```python
import functools
import math

import jax
import jax.numpy as jnp
from jax import lax
from jax.experimental import pallas as pl
from jax.experimental.pallas import tpu as pltpu

D_MODEL = 1024
DEPTH = 4
N_MEM = 256
W = D_MODEL // 2
CONV_A_W = 3
CHUNK = 128
GROUPS = 4
GROUP_W = W // GROUPS
CONV_C_W = 31
HEADS = 4
HEAD_DIM = W // HEADS
D_FF = 4 * D_MODEL
N_BRANCH = 4
IN_COLS = 8 * W + N_BRANCH * D_MODEL
ALPHA = (2 * DEPTH) ** 0.25
LN_EPS = 1e-5
ATTN_SCALE = HEAD_DIM ** -0.5

V7X_VMEM_BYTES = 64 * 1024 * 1024
VMEM_LIMIT = V7X_VMEM_BYTES - 8 * 1024 * 1024
SUBLANES = 8
BF16_SUBLANES = 16
LANES = 128

TM_MIX = 256
TM_MLP = 512
HALO_A = 8
HALO_C = 32
CONV_RB = 64
SEQ_BLK = 16
ATTN_GROUP = 8
SEQ_BLK_MIX = 32

F32 = jnp.float32
BF16 = jnp.bfloat16


def _layer_norm(x, g, b):
    mu = jnp.mean(x, axis=-1, keepdims=True)
    xc = x - mu
    var = jnp.mean(xc * xc, axis=-1, keepdims=True)
    return xc * lax.rsqrt(var + LN_EPS) * g + b


def _dot(a, b):
    return jnp.dot(a, b, preferred_element_type=F32)


def _dot_nt(a, b):
    return lax.dot_general(a, b, (((1,), (1,)), ((), ())), preferred_element_type=F32)


def _gelu_tanh(x):
    c1 = math.sqrt(2.0 / math.pi)
    u = x * (c1 + (c1 * 0.044715) * (x * x))
    h = 0.5 * x
    return h + h * jnp.tanh(u)


def _softmax_rows(s):
    m = jnp.max(s, axis=-1, keepdims=True)
    e = jnp.exp(s - m)
    return e / jnp.sum(e, axis=-1, keepdims=True)


def _merge_and_norm(x, gate_pre, ys, wbr_ref, wo_ref, ln1g_ref, ln1b_ref):
    acc = None
    for n in range(N_BRANCH):
        term = jax.nn.sigmoid(gate_pre(n)) * _dot(ys[n].astype(BF16), wbr_ref[0, n])
        acc = term if acc is None else acc + term
    mix = _dot(acc.astype(BF16), wo_ref[0])
    return _layer_norm(ALPHA * x + mix, ln1g_ref[0], ln1b_ref[0])


def _kv_kernel(mem_ref, w_ref, k_ref, v_ref):
    rows = mem_ref.shape[0]
    kv = _dot(mem_ref[...].astype(BF16), w_ref[0].astype(BF16))
    for h in range(HEADS):
        k_ref[0, pl.ds(h, rows, stride=HEADS), :] = kv[:, h * HEAD_DIM:(h + 1) * HEAD_DIM]
        v_ref[0, pl.ds(h, rows, stride=HEADS), :] = kv[:, W + h * HEAD_DIM:W + (h + 1) * HEAD_DIM]


def _kv_project(mem2d, w_kv):
    rows = mem2d.shape[0]
    return pl.pallas_call(
        _kv_kernel,
        grid=(DEPTH,),
        in_specs=[
            pl.BlockSpec((rows, D_MODEL), lambda l: (0, 0)),
            pl.BlockSpec((1, D_MODEL, 2 * W), lambda l: (l, 0, 0)),
        ],
        out_specs=[
            pl.BlockSpec((1, rows * HEADS, HEAD_DIM), lambda l: (l, 0, 0)),
            pl.BlockSpec((1, rows * HEADS, HEAD_DIM), lambda l: (l, 0, 0)),
        ],
        out_shape=[jax.ShapeDtypeStruct((DEPTH, rows * HEADS, HEAD_DIM), F32)] * 2,
        compiler_params=pltpu.CompilerParams(
            dimension_semantics=("arbitrary",), vmem_limit_bytes=VMEM_LIMIT),
        name="kv_project",
    )(mem2d, w_kv)


def _conv_c_rows(ext_c, wb_ref, r0):
    lead = HALO_C - (CONV_C_W - 1)
    span = CONV_RB + HALO_C
    cols = []
    for c in range(W // LANES):
        lanes = slice(c * LANES, (c + 1) * LANES)
        win = ext_c[r0:r0 + span, lanes]
        acc = None
        for b in range(SUBLANES):
            sh = win if b == 0 else pltpu.roll(win, span - b, axis=0)
            for a in range(HALO_C // SUBLANES + 1):
                k = a * SUBLANES + b - lead
                if 0 <= k < CONV_C_W:
                    tap = sh[a * SUBLANES:a * SUBLANES + CONV_RB].reshape(CONV_RB // SUBLANES, SUBLANES, LANES)
                    term = wb_ref[k * SUBLANES:(k + 1) * SUBLANES, lanes][None] * tap
                    acc = term if acc is None else acc + term
        cols.append(acc.reshape(CONV_RB, LANES))
    return jnp.concatenate(cols, axis=1)


def _mixer_stage(z_ref, x_ref, mk_ref, mv_ref, wca_ref, lnvg_ref, lnvb_ref, ws_ref, bst_ref,
                 bcc_ref, lncg_ref, lncb_ref, wbr_ref, wo_ref, ln1g_ref, ln1b_ref,
                 y_ref, nba_ref, nbc_ref, ext_a, ext_c, wb_c):
    tm = z_ref.shape[0]
    n_chunks = tm // CHUNK

    glu = z_ref[:, 5 * W:6 * W] * jax.nn.sigmoid(z_ref[:, 6 * W:7 * W])
    ext_c[HALO_C:HALO_C + tm, :] = glu
    conv_c = jnp.concatenate(
        [_conv_c_rows(ext_c, wb_c, rb * CONV_RB) for rb in range(tm // CONV_RB)], axis=0) + bcc_ref[0]
    nbc_ref[0] = ext_c[HALO_C + tm - (CONV_C_W - 1):HALO_C + tm, :]
    ext_c[0:HALO_C, :] = ext_c[tm:tm + HALO_C, :]
    y_c = jax.nn.silu(_layer_norm(conv_c, lncg_ref[0], lncb_ref[0]))

    p = z_ref[:, 2 * W:3 * W] * z_ref[:, 0:W]
    ext_a[HALO_A:HALO_A + tm, :] = p
    wca = wca_ref[0]
    conv_a = (wca[0:1, :] * ext_a[HALO_A - 2:HALO_A - 2 + tm, :]
              + wca[1:2, :] * ext_a[HALO_A - 1:HALO_A - 1 + tm, :]
              + wca[2:3, :] * p)
    y_a = z_ref[:, W:2 * W] * conv_a
    nba_ref[0] = ext_a[HALO_A + tm - 2:HALO_A + tm, :]
    ext_a[0:HALO_A, :] = ext_a[tm:tm + HALO_A, :]

    u = _gelu_tanh(z_ref[:, 3 * W:4 * W])
    v = _layer_norm(_gelu_tanh(z_ref[:, 4 * W:5 * W]), lnvg_ref[0], lnvb_ref[0]).astype(BF16)
    row = lax.broadcasted_iota(jnp.int32, (CHUNK, CHUNK), 0)
    col = lax.broadcasted_iota(jnp.int32, (CHUNK, CHUNK), 1)
    causal = row >= col
    gate_cols = []
    for g in range(GROUPS):
        wg = jnp.where(causal, ws_ref[0, g], 0.0).astype(BF16)
        rhs = jnp.concatenate(
            [v[n * CHUNK:(n + 1) * CHUNK, g * GROUP_W:(g + 1) * GROUP_W] for n in range(n_chunks)],
            axis=1)
        gate_cols.append(_dot(wg, rhs) + bst_ref[0, :, g:g + 1])
    s = jnp.concatenate(
        [jnp.concatenate([gate_cols[g][:, n * GROUP_W:(n + 1) * GROUP_W] for g in range(GROUPS)], axis=1)
         for n in range(n_chunks)], axis=0)
    y_b = u * s

    scores = []
    for h in range(HEADS):
        qh = z_ref[:, 7 * W + h * HEAD_DIM:7 * W + (h + 1) * HEAD_DIM].astype(BF16)
        kh = mk_ref[0, 0, pl.ds(h, N_MEM, stride=HEADS), :].astype(BF16)
        scores.append(_dot_nt(qh, kh))
    pr = _softmax_rows(jnp.concatenate(scores, axis=0) * ATTN_SCALE).astype(BF16)
    heads = []
    for h in range(HEADS):
        vh = mv_ref[0, 0, pl.ds(h, N_MEM, stride=HEADS), :].astype(BF16)
        heads.append(_dot(pr[h * tm:(h + 1) * tm], vh))
    y_x = jnp.concatenate(heads, axis=1)

    gate_pre = lambda n: z_ref[:, 8 * W + n * D_MODEL:8 * W + (n + 1) * D_MODEL]
    y_ref[0] = _merge_and_norm(x_ref[0], gate_pre, (y_a, y_b, y_c, y_x), wbr_ref, wo_ref,
                               ln1g_ref, ln1b_ref)


def _prompt_mixer_kernel(x_ref, mk_ref, mv_ref, w_in_ref, wca_ref, lnvg_ref, lnvb_ref,
                         ws_ref, bst_ref, wcc_ref, bcc_ref, lncg_ref, lncb_ref, wbr_ref, wo_ref,
                         ln1g_ref, ln1b_ref, wu32_ref, wd32_ref,
                         y_ref, nba_ref, nbc_ref, wu16_ref, wd16_ref, z_ref, ext_a, ext_c, wb_c):
    wu16_ref[...] = wu32_ref[...].astype(BF16)
    wd16_ref[...] = wd32_ref[...].astype(BF16)

    @pl.when(pl.program_id(1) == 0)
    def _():
        ext_a[0:HALO_A, :] = jnp.zeros((HALO_A, W), F32)
        ext_c[0:HALO_C, :] = jnp.zeros((HALO_C, W), F32)
        for k in range(CONV_C_W):
            wb_c[k * SUBLANES:(k + 1) * SUBLANES, :] = jnp.broadcast_to(wcc_ref[0, k:k + 1, :], (SUBLANES, W))

    x = x_ref[0]
    xb = x.astype(BF16)
    for c0, c1 in ((5 * W, 7 * W), (0, 3 * W), (3 * W, 5 * W), (7 * W, 8 * W)):
        z_ref[:, c0:c1] = _dot(xb, w_in_ref[0, :, c0:c1])
    for c0 in range(8 * W, IN_COLS, D_MODEL):
        z_ref[:, c0:c0 + D_MODEL] = _dot(xb, w_in_ref[0, :, c0:c0 + D_MODEL])
    _mixer_stage(z_ref, x_ref, mk_ref, mv_ref, wca_ref, lnvg_ref, lnvb_ref, ws_ref,
                 bst_ref, bcc_ref, lncg_ref, lncb_ref, wbr_ref, wo_ref, ln1g_ref, ln1b_ref,
                 y_ref, nba_ref, nbc_ref, ext_a, ext_c, wb_c)


def _resident(shape_tail, layer):
    nd = len(shape_tail)
    return pl.BlockSpec((1,) + tuple(shape_tail), lambda *_: (layer,) + (0,) * nd,
                        pipeline_mode=pl.Buffered(1))


def _prompt_mixer(layer, x, mk, mv, p, big, w_up, w_down):
    bn, seq, _ = x.shape
    tm = TM_MIX
    tiles = seq // tm
    n_steps = bn * tiles
    up_rows, down_rows = D_MODEL // n_steps, D_FF // n_steps
    assert up_rows * n_steps == D_MODEL and up_rows % BF16_SUBLANES == 0
    return pl.pallas_call(
        _prompt_mixer_kernel,
        grid=(bn, tiles),
        in_specs=[
            pl.BlockSpec((1, tm, D_MODEL), lambda b, j: (b, j, 0)),
            pl.BlockSpec((1, 1, N_MEM * HEADS, HEAD_DIM), lambda b, j: (layer, b, 0, 0)),
            pl.BlockSpec((1, 1, N_MEM * HEADS, HEAD_DIM), lambda b, j: (layer, b, 0, 0)),
            _resident((D_MODEL, IN_COLS), 0),
            _resident((CONV_A_W, W), layer),
            _resident((1, W), layer),
            _resident((1, W), layer),
            _resident((GROUPS, CHUNK, CHUNK), layer),
            _resident((CHUNK, GROUPS), layer),
            _resident((CONV_C_W, W), layer),
            _resident((1, W), layer),
            _resident((1, W), layer),
            _resident((1, W), layer),
            _resident((N_BRANCH, W, D_MODEL), 0),
            _resident((D_MODEL, D_MODEL), 0),
            _resident((1, D_MODEL), layer),
            _resident((1, D_MODEL), layer),
            pl.BlockSpec((1, up_rows, D_FF), lambda b, j: (layer, b * tiles + j, 0)),
            pl.BlockSpec((1, down_rows, D_MODEL), lambda b, j: (layer, b * tiles + j, 0)),
        ],
        out_specs=[
            pl.BlockSpec((1, tm, D_MODEL), lambda b, j: (b, j, 0)),
            pl.BlockSpec((1, CONV_A_W - 1, W), lambda b, j: (b, 0, 0)),
            pl.BlockSpec((1, CONV_C_W - 1, W), lambda b, j: (b, 0, 0)),
            pl.BlockSpec((1, up_rows, D_FF), lambda b, j: (0, b * tiles + j, 0)),
            pl.BlockSpec((1, down_rows, D_MODEL), lambda b, j: (0, b * tiles + j, 0)),
        ],
        out_shape=[
            jax.ShapeDtypeStruct((bn, seq, D_MODEL), F32),
            jax.ShapeDtypeStruct((bn, CONV_A_W - 1, W), F32),
            jax.ShapeDtypeStruct((bn, CONV_C_W - 1, W), F32),
            jax.ShapeDtypeStruct((1, D_MODEL, D_FF), BF16),
            jax.ShapeDtypeStruct((1, D_FF, D_MODEL), BF16),
        ],
        scratch_shapes=[
            pltpu.VMEM((tm, IN_COLS), F32),
            pltpu.VMEM((tm + HALO_A, W), F32),
            pltpu.VMEM((tm + HALO_C, W), F32),
            pltpu.VMEM((CONV_C_W * SUBLANES, W), F32),
        ],
        compiler_params=pltpu.CompilerParams(
            dimension_semantics=("arbitrary", "arbitrary"), vmem_limit_bytes=VMEM_LIMIT),
        name="prompt_mixer",
    )(x, mk, mv, big["w_in"], p["w_conv_a"], p["ln_v_g"], p["ln_v_b"], p["w_s"], p["b_s_t"],
      p["w_conv_c"], p["b_conv_c"], p["ln_c_g"], p["ln_c_b"], big["w_out_br"], big["w_o"],
      p["ln1_g"], p["ln1_b"], w_up, w_down)


def _sample_attn_kernel(x_ref, wq_ref, k_ref, v_ref, o_ref):
    n_pos = x_ref.shape[0]
    sb = ATTN_GROUP
    rows = n_pos * sb
    r_seq = lax.broadcasted_iota(jnp.int32, (HEADS * rows, sb * N_MEM), 0) % sb
    c_seq = lax.broadcasted_iota(jnp.int32, (HEADS * rows, sb * N_MEM), 1) // N_MEM
    own = r_seq == c_seq
    for g0 in range(0, x_ref.shape[1], sb):
        xb = x_ref[:, g0:g0 + sb, :].reshape(rows, D_MODEL).astype(BF16)
        q = _dot(xb, wq_ref[0])
        scores = []
        for h in range(HEADS):
            sl = slice(h * HEAD_DIM, (h + 1) * HEAD_DIM)
            kh = k_ref[g0:g0 + sb, pl.ds(h, N_MEM, stride=HEADS), :].reshape(sb * N_MEM, HEAD_DIM)
            scores.append(_dot_nt(q[:, sl].astype(BF16), kh.astype(BF16)))
        sc = jnp.where(own, jnp.concatenate(scores, axis=0) * ATTN_SCALE, -jnp.inf)
        pr = _softmax_rows(sc).astype(BF16)
        heads = []
        for h in range(HEADS):
            vh = v_ref[g0:g0 + sb, pl.ds(h, N_MEM, stride=HEADS), :].reshape(sb * N_MEM, HEAD_DIM)
            heads.append(_dot(pr[h * rows:(h + 1) * rows], vh.astype(BF16)))
        o_ref[:, g0:g0 + sb, :] = jnp.concatenate(heads, axis=1).reshape(n_pos, sb, W)


def _sample_attn(layer, x_t, w_in_bf, cache_k, cache_v):
    n_pos, n_seq, _ = x_t.shape
    sb = SEQ_BLK
    q_block = (7 * W) // W
    return pl.pallas_call(
        _sample_attn_kernel,
        grid=(n_seq // sb,),
        in_specs=[
            pl.BlockSpec((n_pos, sb, D_MODEL), lambda i: (0, i, 0)),
            pl.BlockSpec((1, D_MODEL, W), lambda i: (0, 0, q_block)),
            pl.BlockSpec((None, sb, N_MEM * HEADS, HEAD_DIM), lambda i: (layer, i, 0, 0)),
            pl.BlockSpec((None, sb, N_MEM * HEADS, HEAD_DIM), lambda i: (layer, i, 0, 0)),
        ],
        out_specs=pl.BlockSpec((n_pos, sb, W), lambda i: (0, i, 0)),
        out_shape=jax.ShapeDtypeStruct((n_pos, n_seq, W), F32),
        compiler_params=pltpu.CompilerParams(
            dimension_semantics=("arbitrary",), vmem_limit_bytes=VMEM_LIMIT),
        name="sample_attn",
    )(x_t, w_in_bf, cache_k, cache_v)


def _sample_mixer_kernel(x_ref, yx_ref, sa_ref, sc_ref, w_in_ref, wca_ref, lnvg_ref, lnvb_ref,
                         wsr_ref, bsr_ref, wcc_ref, bcc_ref, lncg_ref, lncb_ref, wbr_ref, wo_ref,
                         ln1g_ref, ln1b_ref,
                         y_ref, p_ref, nsc_ref, v_ref):
    n_pos, n_seq, _ = x_ref.shape
    rows = n_pos * n_seq
    x = x_ref[...].reshape(rows, D_MODEL)
    xb = x.astype(BF16)

    def proj(c0, c1):
        return _dot(xb, w_in_ref[0, :, c0:c1])

    def pos(a, t):
        return a[t * n_seq:(t + 1) * n_seq, :]

    za = proj(0, 3 * W)
    p = za[:, 2 * W:3 * W] * za[:, 0:W]
    p_ref[...] = p.reshape(n_pos, n_seq, W)
    wca = wca_ref[0]
    hist_a = [sa_ref[0, r] for r in range(CONV_A_W - 1)] + [pos(p, t) for t in range(n_pos)]
    conv_a = jnp.concatenate(
        [sum(wca[k:k + 1, :] * hist_a[t + k] for k in range(CONV_A_W)) for t in range(n_pos)], axis=0)
    y_a = za[:, W:2 * W] * conv_a

    zb = _gelu_tanh(proj(3 * W, 5 * W))
    u = zb[:, :W]
    v = _layer_norm(zb[:, W:], lnvg_ref[0], lnvb_ref[0])
    v_ref[...] = v.reshape(n_pos, n_seq, W)
    gate_rows = []
    for t in range(n_pos):
        acc = bsr_ref[0, t:t + 1, :]
        for s_ in range(t + 1):
            acc = acc + wsr_ref[0, t * n_pos + s_:t * n_pos + s_ + 1, :] * pos(v, s_)
        gate_rows.append(acc)
    y_b = u * jnp.concatenate(gate_rows, axis=0)

    zc = proj(5 * W, 7 * W)
    glu = zc[:, :W] * jax.nn.sigmoid(zc[:, W:])
    keep = CONV_C_W - 1 - n_pos
    nsc_ref[0:keep] = sc_ref[0, n_pos:CONV_C_W - 1]
    nsc_ref[keep:CONV_C_W - 1] = glu.reshape(n_pos, n_seq, W)
    wcc = wcc_ref[0]
    conv_rows = []
    for t in range(n_pos):
        acc = None
        for k in range(CONV_C_W):
            r = t + k
            src = sc_ref[0, r] if r < CONV_C_W - 1 else pos(glu, r - (CONV_C_W - 1))
            term = wcc[k:k + 1, :] * src
            acc = term if acc is None else acc + term
        conv_rows.append(acc)
    conv_c = jnp.concatenate(conv_rows, axis=0) + bcc_ref[0]
    y_c = jax.nn.silu(_layer_norm(conv_c, lncg_ref[0], lncb_ref[0]))

    y_x = yx_ref[...].reshape(rows, W)
    gate_pre = lambda n: proj(8 * W + n * D_MODEL, 8 * W + (n + 1) * D_MODEL)
    y = _merge_and_norm(x, gate_pre, (y_a, y_b, y_c, y_x), wbr_ref, wo_ref, ln1g_ref, ln1b_ref)
    y_ref[...] = y.reshape(n_pos, n_seq, D_MODEL)


def _sample_mixer(layer, x_t, yx_t, state_a_t, state_c_t, p, big):
    n_pos, n_seq, _ = x_t.shape
    sb = SEQ_BLK_MIX

    def by_pos(width):
        return pl.BlockSpec((n_pos, sb, width), lambda i: (0, i, 0))

    layer_slice = functools.partial(_resident, layer=layer)

    return pl.pallas_call(
        _sample_mixer_kernel,
        grid=(n_seq // sb,),
        in_specs=[
            by_pos(D_MODEL),
            by_pos(W),
            pl.BlockSpec((1, CONV_A_W - 1, sb, W), lambda i: (layer, 0, i, 0)),
            pl.BlockSpec((1, CONV_C_W - 1, sb, W), lambda i: (layer, 0, i, 0)),
            _resident((D_MODEL, IN_COLS), 0),
            layer_slice((CONV_A_W, W)),
            layer_slice((1, W)),
            layer_slice((1, W)),
            layer_slice((n_pos * n_pos, W)),
            layer_slice((n_pos, W)),
            layer_slice((CONV_C_W, W)),
            layer_slice((1, W)),
            layer_slice((1, W)),
            layer_slice((1, W)),
            _resident((N_BRANCH, W, D_MODEL), 0),
            _resident((D_MODEL, D_MODEL), 0),
            layer_slice((1, D_MODEL)),
            layer_slice((1, D_MODEL)),
        ],
        out_specs=[
            by_pos(D_MODEL),
            by_pos(W),
            pl.BlockSpec((CONV_C_W - 1, sb, W), lambda i: (0, i, 0)),
            by_pos(W),
        ],
        out_shape=[
            jax.ShapeDtypeStruct((n_pos, n_seq, D_MODEL), F32),
            jax.ShapeDtypeStruct((n_pos, n_seq, W), F32),
            jax.ShapeDtypeStruct((CONV_C_W - 1, n_seq, W), F32),
            jax.ShapeDtypeStruct((n_pos, n_seq, W), F32),
        ],
        compiler_params=pltpu.CompilerParams(
            dimension_semantics=("arbitrary",), vmem_limit_bytes=VMEM_LIMIT),
        name="sample_mixer",
    )(x_t, yx_t, state_a_t, state_c_t, big["w_in"], p["w_conv_a"], p["ln_v_g"], p["ln_v_b"],
      p["w_s_rows"], p["b_s_rows"], p["w_conv_c"], p["b_conv_c"], p["ln_c_g"], p["ln_c_b"],
      big["w_out_br"], big["w_o"], p["ln1_g"], p["ln1_b"])


def _mlp_kernel(x_ref, xs_ref, wu_ref, bu_ref, wd_ref, bd_ref, g_ref, b_ref, *rest):
    n_side = (len(rest) - 2) // 2
    y_ref, ys_ref = rest[n_side], rest[n_side + 1]
    for src, dst in zip(rest[:n_side], rest[n_side + 2:]):
        dst[...] = src[...].astype(BF16)

    def mlp_rows(x):
        h = jnp.maximum(_dot(x.astype(BF16), wu_ref[0]) + bu_ref[0], 0.0)
        h = _dot((h * h).astype(BF16), wd_ref[0]) + bd_ref[0]
        return _layer_norm(ALPHA * x + h, g_ref[0], b_ref[0])

    y_ref[...] = mlp_rows(x_ref[...])

    @pl.when(pl.program_id(0) == pl.num_programs(0) - 1)
    def _():
        ys_ref[...] = mlp_rows(xs_ref[...])


def _mlp(layer, x2d, xs2d, p, wu16, wd16, round_next=()):
    rows = x2d.shape[0]
    rows_s = xs2d.shape[0]
    tm = min(TM_MLP, rows)
    n_steps = rows // tm
    side_in, side_out, side_shape = [], [], []
    for w32, l_next in round_next:
        _, r, c = w32.shape
        slab = r // n_steps
        assert slab * n_steps == r and slab % BF16_SUBLANES == 0
        side_in.append(pl.BlockSpec((1, slab, c), functools.partial(lambda i, l: (l, i, 0), l=l_next)))
        side_out.append(pl.BlockSpec((1, slab, c), lambda i: (0, i, 0)))
        side_shape.append(jax.ShapeDtypeStruct((1, r, c), BF16))
    out = pl.pallas_call(
        _mlp_kernel,
        grid=(n_steps,),
        in_specs=[
            pl.BlockSpec((tm, D_MODEL), lambda i: (i, 0)),
            pl.BlockSpec((rows_s, D_MODEL), lambda i: (0, 0), pipeline_mode=pl.Buffered(1)),
            _resident((D_MODEL, D_FF), 0),
            _resident((1, D_FF), layer),
            _resident((D_FF, D_MODEL), 0),
            _resident((1, D_MODEL), layer),
            _resident((1, D_MODEL), layer),
            _resident((1, D_MODEL), layer),
        ] + side_in,
        out_specs=[pl.BlockSpec((tm, D_MODEL), lambda i: (i, 0)),
                   pl.BlockSpec((rows_s, D_MODEL), lambda i: (0, 0))] + side_out,
        out_shape=[jax.ShapeDtypeStruct((rows, D_MODEL), F32),
                   jax.ShapeDtypeStruct((rows_s, D_MODEL), F32)] + side_shape,
        compiler_params=pltpu.CompilerParams(
            dimension_semantics=("arbitrary",), vmem_limit_bytes=VMEM_LIMIT),
        name="mlp",
    )(x2d, xs2d, wu16, p["b_up"], wd16, p["b_down"], p["ln2_g"], p["ln2_b"], *[w for w, _ in round_next])
    return out[0], out[1], out[2:]


def kernel(x_prompt, x_sample, mem_prompt, state_conv_a, state_conv_c, cache_mem_k, cache_mem_v,
           w_in, w_conv_a, ln_v_g, ln_v_b, w_s, b_s, w_conv_c, b_conv_c, ln_c_g, ln_c_b,
           w_mem_kv, w_out_br, w_o, ln1_g, ln1_b, w_up, b_up, w_down, b_down, ln2_g, ln2_b):
    bp, seq, _ = x_prompt.shape
    n_seq, n_pos, _ = x_sample.shape
    assert seq % TM_MIX == 0 and TM_MIX % CHUNK == 0 and TM_MIX % CONV_RB == 0
    assert n_seq % SEQ_BLK == 0 and SEQ_BLK % ATTN_GROUP == 0 and n_seq % SEQ_BLK_MIX == 0
    assert CONV_A_W - 1 <= n_pos <= CHUNK

    row = lambda a: a[:, None, :]
    params = {
        "w_conv_a": w_conv_a, "ln_v_g": row(ln_v_g), "ln_v_b": row(ln_v_b),
        "w_s": w_s, "b_s_t": jnp.swapaxes(b_s, 1, 2),
        "w_s_rows": jnp.repeat(
            jnp.transpose(w_s[:, :, :n_pos, :n_pos], (0, 2, 3, 1)).reshape(DEPTH, n_pos * n_pos, GROUPS),
            GROUP_W, axis=-1),
        "b_s_rows": jnp.repeat(jnp.swapaxes(b_s[:, :, :n_pos], 1, 2), GROUP_W, axis=-1),
        "w_conv_c": w_conv_c, "b_conv_c": row(b_conv_c), "ln_c_g": row(ln_c_g), "ln_c_b": row(ln_c_b),
        "ln1_g": row(ln1_g), "ln1_b": row(ln1_b),
        "b_up": row(b_up), "b_down": row(b_down), "ln2_g": row(ln2_g), "ln2_b": row(ln2_b),
    }
    w_br2d = w_out_br.reshape(DEPTH, N_BRANCH * W, D_MODEL)
    big = {"w_in": w_in[0:1].astype(BF16), "w_out_br": w_out_br[0:1].astype(BF16),
           "w_o": w_o[0:1].astype(BF16)}

    pk, pv = _kv_project(mem_prompt.reshape(bp * N_MEM, D_MODEL), w_mem_kv)
    pk = pk.reshape(DEPTH, bp, N_MEM * HEADS, HEAD_DIM)
    pv = pv.reshape(DEPTH, bp, N_MEM * HEADS, HEAD_DIM)

    ys = jnp.swapaxes(x_sample, 0, 1)
    state_a_t = jnp.swapaxes(state_conv_a, 1, 2)
    state_c_t = jnp.swapaxes(state_conv_c, 1, 2)
    cache_k = cache_mem_k.reshape(DEPTH, n_seq, N_MEM * HEADS, HEAD_DIM)
    cache_v = cache_mem_v.reshape(DEPTH, n_seq, N_MEM * HEADS, HEAD_DIM)

    yp = x_prompt
    pa, pc, sa, sc, sv = [], [], [], [], []
    for l in range(DEPTH):
        x1, nba, nbc, wu16, wd16 = _prompt_mixer(l, yp, pk, pv, params, big, w_up, w_down)
        pa.append(nba)
        pc.append(nbc)

        yx = _sample_attn(l, ys, big["w_in"], cache_k, cache_v)
        s1, p_t, nsc_t, v_t = _sample_mixer(l, ys, yx, state_a_t, state_c_t, params, big)

        nxt = [(w_in, l + 1), (w_br2d, l + 1), (w_o, l + 1)] if l + 1 < DEPTH else []
        yp, ys, rounded = _mlp(l, x1.reshape(bp * seq, D_MODEL), s1.reshape(n_pos * n_seq, D_MODEL),
                               params, wu16, wd16, nxt)
        yp = yp.reshape(bp, seq, D_MODEL)
        ys = ys.reshape(n_pos, n_seq, D_MODEL)
        sa.append(jnp.swapaxes(p_t[n_pos - (CONV_A_W - 1):], 0, 1))
        sc.append(nsc_t)
        sv.append(jnp.swapaxes(v_t, 0, 1))
        if rounded:
            big = {"w_in": rounded[0], "w_out_br": rounded[1].reshape(1, N_BRANCH, W, D_MODEL),
                   "w_o": rounded[2]}

    return (yp, jnp.swapaxes(ys, 0, 1), jnp.stack(pa), jnp.stack(pc),
            pk.reshape(DEPTH, bp, N_MEM, HEADS, HEAD_DIM), pv.reshape(DEPTH, bp, N_MEM, HEADS, HEAD_DIM),
            jnp.stack(sa), jnp.swapaxes(jnp.stack(sc), 1, 2), jnp.stack(sv))
```
